```python
import jax, jax.numpy as jnp

D_MODEL = 2048
BATCH = 4
SEQ = 4096
DEPTH = 1

CTX_LEN = 256
GRID_W = 64

HEAD_DIM = 128
N_HEADS = 8
N_KV_HEADS = 2
ATTN_WIDTH = N_HEADS * HEAD_DIM
KV_WIDTH = N_KV_HEADS * HEAD_DIM
WINDOW = 128
ATTN_BLOCK = 128
ROPE_BASE = 10000.0

CHUNK = 128
N_GMLP_GROUPS = 8
GMLP_WIDTH = 1024
GMLP_GROUP_DIM = GMLP_WIDTH // N_GMLP_GROUPS

D_FF = 5632
FFN_RES_WEIGHT = 0.5
N_MOD = 9
NORM_EPS = 1e-6
NEG_INF = -1e30

PROJ_WIDTH = ATTN_WIDTH + 2 * KV_WIDTH + 2 * GMLP_WIDTH + 2 * D_MODEL

kernel_name = "hybrid_dit_gmlp_window_gqa_macaron"


def _rmsnorm(t, g):
    tf = t.astype(jnp.float32)
    y = tf * jax.lax.rsqrt(jnp.mean(tf * tf, axis=-1, keepdims=True) + NORM_EPS)
    return (y * g.astype(jnp.float32)).astype(t.dtype)


def _layernorm(t, g, b):
    tf = t.astype(jnp.float32)
    mu = jnp.mean(tf, axis=-1, keepdims=True)
    var = jnp.mean(jnp.square(tf - mu), axis=-1, keepdims=True)
    y = (tf - mu) * jax.lax.rsqrt(var + NORM_EPS)
    return (y * g.astype(jnp.float32) + b.astype(jnp.float32)).astype(t.dtype)


def _modulate(t, g_pre, shift, scale):
    return _rmsnorm(t, g_pre) * (1.0 + scale) + shift


def _swiglu(h, w_in, w_out):
    gt, up = jnp.split(h @ w_in, 2, axis=-1)
    return (jax.nn.silu(gt) * up) @ w_out


def _ffn_step(t, mod, j, g, w_in, w_out):
    h = _modulate(t, g[2 * j], mod[:, 3 * j], mod[:, 3 * j + 1])
    y = _rmsnorm(_swiglu(h, w_in, w_out), g[2 * j + 1])
    return t + FFN_RES_WEIGHT * mod[:, 3 * j + 2] * y


def _axial_rope_tables(rows):
    row = jnp.repeat(jnp.arange(rows), GRID_W).astype(jnp.float32)
    col = jnp.tile(jnp.arange(GRID_W), rows).astype(jnp.float32)
    quarter = HEAD_DIM // 4
    freqs = ROPE_BASE ** (-jnp.arange(quarter, dtype=jnp.float32) / quarter)
    ang_r = row[:, None] * freqs[None, :]
    ang_c = col[:, None] * freqs[None, :]
    ang = jnp.concatenate([ang_r, ang_r, ang_c, ang_c], axis=-1)
    return jnp.cos(ang)[:, None, :], jnp.sin(ang)[:, None, :]


def _rope(t, cos, sin):
    ts = t.reshape(t.shape[:-1] + (2, 2, HEAD_DIM // 4))
    rot = jnp.concatenate([-ts[..., 1:, :], ts[..., :1, :]], axis=-2).reshape(t.shape)
    return (t.astype(jnp.float32) * cos + rot.astype(jnp.float32) * sin).astype(t.dtype)


def _chunk_mlp(u, v, ln_g, ln_b, w_s, b_s):
    bsz, n, _ = u.shape
    v = _layernorm(v, ln_g, ln_b)
    vc = v.reshape(bsz, n // CHUNK, CHUNK, N_GMLP_GROUPS, GMLP_GROUP_DIM)
    mixed = jnp.einsum('gpq,bnqgd->bnpgd', w_s, vc) + b_s.T[None, None, :, :, None]
    return u * mixed.reshape(bsz, n, GMLP_WIDTH)


def _window_attention(q, k, v, k_ctx, v_ctx, sink):
    bsz, n = q.shape[0], q.shape[1]
    nb = n // ATTN_BLOCK
    rep = N_HEADS // N_KV_HEADS
    c_len = k_ctx.shape[1]
    qb = q.reshape(bsz, nb, ATTN_BLOCK, N_KV_HEADS, rep, HEAD_DIM)

    def band(t):
        tp = jnp.pad(t, ((0, 0), (ATTN_BLOCK, ATTN_BLOCK), (0, 0), (0, 0)))
        tp = tp.reshape(bsz, nb + 2, ATTN_BLOCK, N_KV_HEADS, HEAD_DIM)
        return jnp.concatenate([tp[:, :-2], tp[:, 1:-1], tp[:, 2:]], axis=2)

    kb, vb = band(k), band(v)
    scale = HEAD_DIM ** -0.5
    s_loc = jnp.einsum('bnqgrd,bnkgd->bngrqk', qb, kb).astype(jnp.float32) * scale
    blk = jnp.arange(nb)[:, None, None]
    qpos = blk * ATTN_BLOCK + jnp.arange(ATTN_BLOCK)[None, :, None]
    kpos = (blk - 1) * ATTN_BLOCK + jnp.arange(3 * ATTN_BLOCK)[None, None, :]
    valid = (jnp.abs(qpos - kpos) <= WINDOW) & (kpos >= 0) & (kpos < n)
    s_loc = jnp.where(valid[None, :, None, None], s_loc, NEG_INF)
    s_ctx = jnp.einsum('bnqgrd,bcgd->bngrqc', qb, k_ctx).astype(jnp.float32) * scale
    s_sink = jnp.broadcast_to(sink.astype(jnp.float32).reshape(1, 1, N_KV_HEADS, rep, 1, 1),
                              s_loc.shape[:-1] + (1,))
    p = jax.nn.softmax(jnp.concatenate([s_loc, s_ctx, s_sink], axis=-1), axis=-1)
    p_loc = p[..., :3 * ATTN_BLOCK].astype(v.dtype)
    p_ctx = p[..., 3 * ATTN_BLOCK:3 * ATTN_BLOCK + c_len].astype(v.dtype)
    out = (jnp.einsum('bngrqk,bnkgd->bnqgrd', p_loc, vb)
           + jnp.einsum('bngrqc,bcgd->bnqgrd', p_ctx, v_ctx))
    return out.reshape(bsz, n, ATTN_WIDTH)


def _context_attention(q, k, v, sink):
    bsz, c_len = q.shape[0], q.shape[1]
    rep = N_HEADS // N_KV_HEADS
    qg = q.reshape(bsz, c_len, N_KV_HEADS, rep, HEAD_DIM)
    s = jnp.einsum('bqgrd,bkgd->bgrqk', qg, k).astype(jnp.float32) * (HEAD_DIM ** -0.5)
    s_sink = jnp.broadcast_to(sink.astype(jnp.float32).reshape(1, N_KV_HEADS, rep, 1, 1),
                              s.shape[:-1] + (1,))
    p = jax.nn.softmax(jnp.concatenate([s, s_sink], axis=-1), axis=-1)[..., :c_len].astype(v.dtype)
    return jnp.einsum('bgrqk,bkgd->bqgrd', p, v).reshape(bsz, c_len, ATTN_WIDTH)


def _project(t, w_in):
    bsz, n = t.shape[0], t.shape[1]
    p = t @ w_in
    o1 = ATTN_WIDTH
    o2 = o1 + KV_WIDTH
    o3 = o2 + KV_WIDTH
    o4 = o3 + GMLP_WIDTH
    o5 = o4 + GMLP_WIDTH
    o6 = o5 + D_MODEL
    q, k, v, u, vg, ga, gb = jnp.split(p, [o1, o2, o3, o4, o5, o6], axis=-1)
    q = q.reshape(bsz, n, N_HEADS, HEAD_DIM)
    k = k.reshape(bsz, n, N_KV_HEADS, HEAD_DIM)
    v = v.reshape(bsz, n, N_KV_HEADS, HEAD_DIM)
    return q, k, v, jax.nn.gelu(u), jax.nn.gelu(vg), ga, gb


def _merge(mix_a, mix_b, ga, gb, w_br_a, w_br_b, w_out):
    merged = jax.nn.sigmoid(ga) * (mix_a @ w_br_a) + jax.nn.sigmoid(gb) * (mix_b @ w_br_b)
    return merged @ w_out


def _mixer(h, h_ctx, w_in, ln_g, ln_b, w_s, b_s, sink, w_br_a, w_br_b, w_out, cos, sin, with_ctx_out):
    q_c, k_c, v_c, u_c, vg_c, ga_c, gb_c = _project(h_ctx, w_in)
    q, k, v, u, vg, ga, gb = _project(h, w_in)
    q, k = _rope(q, cos, sin), _rope(k, cos, sin)
    mix_a = _chunk_mlp(u, vg, ln_g, ln_b, w_s, b_s)
    mix_b = _window_attention(q, k, v, k_c, v_c, sink)
    y = _merge(mix_a, mix_b, ga, gb, w_br_a, w_br_b, w_out)
    y_ctx = None
    if with_ctx_out:
        mix_a_c = _chunk_mlp(u_c, vg_c, ln_g, ln_b, w_s, b_s)
        mix_b_c = _context_attention(q_c, k_c, v_c, sink)
        y_ctx = _merge(mix_a_c, mix_b_c, ga_c, gb_c, w_br_a, w_br_b, w_out)
    return y, y_ctx


def setup_inputs(seed: int = 0) -> dict:
    key = jax.random.key(seed)
    ks = jax.random.split(key, 20)
    f32 = jnp.float32
    L, D, F = DEPTH, D_MODEL, D_FF
    nrm = lambda k, shape, s: jax.random.normal(k, shape, f32) * s
    return {
        "x": nrm(ks[0], (BATCH, SEQ, D), 1.0),
        "c": nrm(ks[1], (BATCH, D), 1.0),
        "ctx": nrm(ks[2], (BATCH, CTX_LEN, D), 1.0),
        "c_ctx": nrm(ks[3], (D,), 1.0),
        "w_ada": nrm(ks[4], (L, D, N_MOD * D), 0.5 * D ** -0.5),
        "b_ada": nrm(ks[5], (L, N_MOD * D), 0.02),
        "norm_g": 1.0 + nrm(ks[6], (L, 6, D), 0.1),
        "w_ffn_in": nrm(ks[7], (L, 2, D, 2 * F), D ** -0.5),
        "w_ffn_out": nrm(ks[8], (L, 2, F, D), F ** -0.5),
        "w_in": nrm(ks[9], (L, D, PROJ_WIDTH), D ** -0.5),
        "gmlp_ln_g": 1.0 + nrm(ks[10], (L, GMLP_WIDTH), 0.1),
        "gmlp_ln_b": nrm(ks[11], (L, GMLP_WIDTH), 0.02),
        "w_spatial": nrm(ks[12], (L, N_GMLP_GROUPS, CHUNK, CHUNK), CHUNK ** -0.5),
        "b_spatial": 1.0 + nrm(ks[13], (L, N_GMLP_GROUPS, CHUNK), 0.1),
        "attn_sink": nrm(ks[14], (L, N_HEADS), 1.0),
        "w_branch_a": nrm(ks[15], (L, GMLP_WIDTH, D), GMLP_WIDTH ** -0.5),
        "w_branch_b": nrm(ks[16], (L, ATTN_WIDTH, D), ATTN_WIDTH ** -0.5),
        "w_out": nrm(ks[17], (L, D, D), D ** -0.5),
    }


def reference(x, c, ctx, c_ctx, w_ada, b_ada, norm_g, w_ffn_in, w_ffn_out, w_in,
              gmlp_ln_g, gmlp_ln_b, w_spatial, b_spatial, attn_sink,
              w_branch_a, w_branch_b, w_out):
    bsz, n = x.shape[0], x.shape[1]
    rows = n // GRID_W
    cos, sin = _axial_rope_tables(rows)
    cos, sin = cos.astype(jnp.float32), sin.astype(jnp.float32)
    for l in range(DEPTH):
        last = l == DEPTH - 1
        g = norm_g[l]
        mod = (jax.nn.silu(c) @ w_ada[l] + b_ada[l]).reshape(bsz, N_MOD, 1, D_MODEL)
        mod_c = (jax.nn.silu(c_ctx) @ w_ada[l] + b_ada[l]).reshape(1, N_MOD, 1, D_MODEL)

        x = _ffn_step(x, mod, 0, g, w_ffn_in[l, 0], w_ffn_out[l, 0])
        ctx = _ffn_step(ctx, mod_c, 0, g, w_ffn_in[l, 0], w_ffn_out[l, 0])

        h = _modulate(x, g[2], mod[:, 3], mod[:, 4])
        h_ctx = _modulate(ctx, g[2], mod_c[:, 3], mod_c[:, 4])
        y, y_ctx = _mixer(h, h_ctx, w_in[l], gmlp_ln_g[l], gmlp_ln_b[l], w_spatial[l],
                          b_spatial[l], attn_sink[l], w_branch_a[l], w_branch_b[l],
                          w_out[l], cos, sin, not last)
        x = x + mod[:, 5] * _rmsnorm(y, g[3])

        x = _ffn_step(x, mod, 2, g, w_ffn_in[l, 1], w_ffn_out[l, 1])
        if not last:
            ctx = ctx + mod_c[:, 5] * _rmsnorm(y_ctx, g[3])
            ctx = _ffn_step(ctx, mod_c, 2, g, w_ffn_in[l, 1], w_ffn_out[l, 1])
    return x
```

```python
import functools

import jax
import jax.numpy as jnp
from jax import lax
from jax.experimental import pallas as pl
from jax.experimental.pallas import tpu as pltpu

GRID_W = 64
HEAD_DIM = 128
N_HEADS = 8
N_KV_HEADS = 2
HEADS_PER_KV = N_HEADS // N_KV_HEADS
ATTN_WIDTH = N_HEADS * HEAD_DIM
KV_WIDTH = N_KV_HEADS * HEAD_DIM
WINDOW = 128
ATTN_BLOCK = 128
ROPE_BASE = 10000.0
CHUNK = 128
N_GMLP_GROUPS = 8
GMLP_WIDTH = 1024
GMLP_GROUP_DIM = GMLP_WIDTH // N_GMLP_GROUPS
FFN_RES_WEIGHT = 0.5
N_MOD = 9
NORM_EPS = 1e-6
NEG_INF = -1e30

ADA_ROWS = 8
V7X_VMEM_LIMIT = 56 * 1024 * 1024

F32 = jnp.float32
BF16 = jnp.bfloat16


def _params(*sem):
    return pltpu.CompilerParams(dimension_semantics=sem, vmem_limit_bytes=V7X_VMEM_LIMIT)


def _dot(a, b):
    return jnp.dot(a, b, preferred_element_type=F32)


def _rms(t):
    return t * lax.rsqrt(jnp.mean(t * t, axis=-1, keepdims=True) + NORM_EPS)


def _ada_kernel(c_ref, w_ref, b_ref, o_ref):
    c = c_ref[...]
    a = (c * jax.nn.sigmoid(c)).astype(BF16)
    o_ref[...] = _dot(a, w_ref[...].astype(BF16)) + b_ref[...]


def _ada(c_all, w_ada, b_ada, tn):
    d, n = w_ada.shape
    assert n % tn == 0
    return pl.pallas_call(
        _ada_kernel,
        grid=(n // tn,),
        in_specs=[pl.BlockSpec((ADA_ROWS, d), lambda j: (0, 0)),
                  pl.BlockSpec((d, tn), lambda j: (0, j)),
                  pl.BlockSpec((1, tn), lambda j: (0, j))],
        out_specs=pl.BlockSpec((ADA_ROWS, tn), lambda j: (0, j)),
        out_shape=jax.ShapeDtypeStruct((ADA_ROWS, n), F32),
        compiler_params=_params("arbitrary"),
        name="ada",
    )(c_all, w_ada, b_ada.reshape(1, n))


def _ffn_kernel(sub, t_ref, mod_ref, g_ref, wg_ref, wu_ref, wo_ref, o_ref, h_ref, acc_ref):
    j = pl.program_id(1)

    @pl.when(j == 0)
    def _():
        y = _rms(t_ref[...]) * g_ref[2 * sub:2 * sub + 1, :]
        h = y * (1.0 + mod_ref[0, 3 * sub + 1:3 * sub + 2, :]) + mod_ref[0, 3 * sub:3 * sub + 1, :]
        h_ref[...] = h.astype(BF16)
        acc_ref[...] = jnp.zeros_like(acc_ref)

    h = h_ref[...]
    gt = _dot(h, wg_ref[...])
    up = _dot(h, wu_ref[...])
    a = (gt * jax.nn.sigmoid(gt)) * up
    acc_ref[...] += _dot(a.astype(BF16), wo_ref[...])

    @pl.when(j == pl.num_programs(1) - 1)
    def _():
        y = _rms(acc_ref[...]) * g_ref[2 * sub + 1:2 * sub + 2, :]
        o_ref[...] = t_ref[...] + (FFN_RES_WEIGHT * mod_ref[0, 3 * sub + 2:3 * sub + 3, :]) * y


def _ffn(t, mod3, norm_g, w_in, w_out, sub, mod_row, tm, tf):
    n_tok, d = t.shape
    f = w_out.shape[0]
    nf = f // tf
    return pl.pallas_call(
        functools.partial(_ffn_kernel, sub),
        grid=(n_tok // tm, nf),
        in_specs=[pl.BlockSpec((tm, d), lambda i, j: (i, 0)),
                  pl.BlockSpec((1, N_MOD, d), lambda i, j: (mod_row(i), 0, 0)),
                  pl.BlockSpec(norm_g.shape, lambda i, j: (0, 0)),
                  pl.BlockSpec((d, tf), lambda i, j: (0, j)),
                  pl.BlockSpec((d, tf), lambda i, j: (0, nf + j)),
                  pl.BlockSpec((tf, d), lambda i, j: (j, 0))],
        out_specs=pl.BlockSpec((tm, d), lambda i, j: (i, 0)),
        out_shape=jax.ShapeDtypeStruct((n_tok, d), F32),
        scratch_shapes=[pltpu.VMEM((tm, d), BF16), pltpu.VMEM((tm, d), F32)],
        compiler_params=_params("parallel", "arbitrary"),
        name=f"ffn{sub}",
    )(t, mod3, norm_g, w_in, w_in, w_out)


def _rope(x, cos, sin_lo, sin_hi):
    return (x * cos + pltpu.roll(x, HEAD_DIM - HEAD_DIM // 4, 1) * sin_lo
            + pltpu.roll(x, HEAD_DIM // 4, 1) * sin_hi)


def _modulated(t_ref, mod_ref, g_ref):
    y = _rms(t_ref[...]) * g_ref[2:3, :]
    return y * (1.0 + mod_ref[0, 4:5, :]) + mod_ref[0, 3:4, :]


def _proj_kernel(n_q, n_ug, t_ref, mod_ref, g_ref, cos_ref, slo_ref, shi_ref, w_ref,
                 q_ref, kv_ref, ug_ref, gate_ref, h_ref):
    j = pl.program_id(1)

    @pl.when(j == 0)
    def _():
        h_ref[...] = _modulated(t_ref, mod_ref, g_ref).astype(BF16)

    p = _dot(h_ref[...], w_ref[...])
    tn = p.shape[1]

    def rope_heads(n):
        cos, slo, shi = cos_ref[...], slo_ref[...], shi_ref[...]
        return [_rope(p[:, a * HEAD_DIM:(a + 1) * HEAD_DIM], cos, slo, shi) for a in range(n)]

    @pl.when(j < n_q)
    def _():
        q_ref[...] = jnp.concatenate(rope_heads(tn // HEAD_DIM), axis=1).astype(BF16)

    @pl.when(j == n_q)
    def _():
        kv_ref[...] = jnp.concatenate(rope_heads(N_KV_HEADS) + [p[:, KV_WIDTH:]], axis=1).astype(BF16)

    @pl.when((j > n_q) & (j <= n_q + n_ug))
    def _():
        ug_ref[...] = jax.nn.gelu(p).astype(BF16)

    @pl.when(j > n_q + n_ug)
    def _():
        gate_ref[...] = jax.nn.sigmoid(p).astype(BF16)


def _proj(x, mod3, norm_g, cos, sin_lo, sin_hi, w_in, seq, tm, tn):
    n_tok, d = x.shape
    assert tn == 2 * KV_WIDTH and ATTN_WIDTH % tn == 0 and GMLP_WIDTH % tn == 0 and d % tn == 0
    n_q, n_ug, n_gate = ATTN_WIDTH // tn, 2 * GMLP_WIDTH // tn, 2 * d // tn
    n_col = n_q + 1 + n_ug + n_gate
    tiles_per_seq = seq // tm
    hd_spec = pl.BlockSpec((tm, HEAD_DIM), lambda i, j: (i % tiles_per_seq, 0))
    return pl.pallas_call(
        functools.partial(_proj_kernel, n_q, n_ug),
        grid=(n_tok // tm, n_col),
        in_specs=[pl.BlockSpec((tm, d), lambda i, j: (i, 0)),
                  pl.BlockSpec((1, N_MOD, d), lambda i, j: (i // tiles_per_seq, 0, 0)),
                  pl.BlockSpec(norm_g.shape, lambda i, j: (0, 0)),
                  hd_spec, hd_spec, hd_spec,
                  pl.BlockSpec((d, tn), lambda i, j: (0, j))],
        out_specs=[pl.BlockSpec((tm, tn), lambda i, j: (i, jnp.clip(j, 0, n_q - 1))),
                   pl.BlockSpec((tm, tn), lambda i, j: (i, 0)),
                   pl.BlockSpec((tm, tn), lambda i, j: (i, jnp.clip(j - n_q - 1, 0, n_ug - 1))),
                   pl.BlockSpec((tm, tn), lambda i, j: (i, jnp.clip(j - n_q - 1 - n_ug, 0, n_gate - 1)))],
        out_shape=[jax.ShapeDtypeStruct((n_tok, ATTN_WIDTH), BF16),
                   jax.ShapeDtypeStruct((n_tok, 2 * KV_WIDTH), BF16),
                   jax.ShapeDtypeStruct((n_tok, 2 * GMLP_WIDTH), BF16),
                   jax.ShapeDtypeStruct((n_tok, 2 * d), BF16)],
        scratch_shapes=[pltpu.VMEM((tm, d), BF16)],
        compiler_params=_params("parallel", "arbitrary"),
        name="proj",
    )(x, mod3, norm_g, cos, sin_lo, sin_hi, w_in)


def _ctx_kv_kernel(t_ref, mod_ref, g_ref, w_ref, kv_ref):
    kv_ref[...] = _dot(_modulated(t_ref, mod_ref, g_ref).astype(BF16), w_ref[...]).astype(BF16)


def _ctx_kv(ctx, mod3, norm_g, w_in, ctx_row, tm):
    n_tok, d = ctx.shape
    tn = 2 * KV_WIDTH
    return pl.pallas_call(
        _ctx_kv_kernel,
        grid=(n_tok // tm,),
        in_specs=[pl.BlockSpec((tm, d), lambda i: (i, 0)),
                  pl.BlockSpec((1, N_MOD, d), lambda i: (ctx_row, 0, 0)),
                  pl.BlockSpec(norm_g.shape, lambda i: (0, 0)),
                  pl.BlockSpec((d, tn), lambda i: (0, ATTN_WIDTH // tn))],
        out_specs=pl.BlockSpec((tm, tn), lambda i: (i, 0)),
        out_shape=jax.ShapeDtypeStruct((n_tok, tn), BF16),
        compiler_params=_params("parallel"),
        name="ctx_kv",
    )(ctx, mod3, norm_g, w_in)


def _gmlp_kernel(u_ref, v_ref, lg_ref, lb_ref, ws_ref, bs_ref, o_ref):
    tm = u_ref.shape[0]
    for c in range(tm // CHUNK):
        rows = slice(c * CHUNK, (c + 1) * CHUNK)
        v = v_ref[rows, :].astype(F32)
        mu = jnp.mean(v, axis=-1, keepdims=True)
        vc = v - mu
        var = jnp.mean(vc * vc, axis=-1, keepdims=True)
        vn = ((vc * lax.rsqrt(var + NORM_EPS)) * lg_ref[...] + lb_ref[...]).astype(BF16)
        for g in range(N_GMLP_GROUPS):
            cols = slice(g * GMLP_GROUP_DIM, (g + 1) * GMLP_GROUP_DIM)
            mixed = _dot(ws_ref[g], vn[:, cols]) + bs_ref[:, cols]
            o_ref[rows, cols] = (u_ref[rows, cols].astype(F32) * mixed).astype(BF16)


def _gmlp(ug, ln_g, ln_b, w_s, b_full, tm):
    n_tok = ug.shape[0]
    return pl.pallas_call(
        _gmlp_kernel,
        grid=(n_tok // tm,),
        in_specs=[pl.BlockSpec((tm, GMLP_WIDTH), lambda i: (i, 0)),
                  pl.BlockSpec((tm, GMLP_WIDTH), lambda i: (i, 1)),
                  pl.BlockSpec((1, GMLP_WIDTH), lambda i: (0, 0)),
                  pl.BlockSpec((1, GMLP_WIDTH), lambda i: (0, 0)),
                  pl.BlockSpec(w_s.shape, lambda i: (0, 0, 0)),
                  pl.BlockSpec(b_full.shape, lambda i: (0, 0))],
        out_specs=pl.BlockSpec((tm, GMLP_WIDTH), lambda i: (i, 0)),
        out_shape=jax.ShapeDtypeStruct((n_tok, GMLP_WIDTH), BF16),
        compiler_params=_params("parallel"),
        name="gmlp",
    )(ug, ug, ln_g.reshape(1, -1), ln_b.reshape(1, -1), w_s, b_full)


def _attn_kernel(sink_ref, q_ref, kvp_ref, kvc_ref, kvn_ref, kvx_ref, o_ref):
    blk = pl.program_id(1)
    nb = pl.num_programs(1)
    blk_q = q_ref.shape[0]
    rows = HEADS_PER_KV * blk_q
    qi = lax.broadcasted_iota(jnp.int32, (rows, ATTN_BLOCK), 0) % blk_q
    kj = lax.broadcasted_iota(jnp.int32, (rows, ATTN_BLOCK), 1)
    valid_prev = (kj >= qi) & (blk > 0)
    valid_next = (kj <= qi) & (blk < nb - 1)
    scale = HEAD_DIM ** -0.5
    nt = (((1,), (1,)), ((), ()))
    for g in range(N_KV_HEADS):
        kc = slice(g * HEAD_DIM, (g + 1) * HEAD_DIM)
        vc = slice(KV_WIDTH + g * HEAD_DIM, KV_WIDTH + (g + 1) * HEAD_DIM)
        heads = range(g * HEADS_PER_KV, (g + 1) * HEADS_PER_KV)
        q = jnp.concatenate([q_ref[:, a * HEAD_DIM:(a + 1) * HEAD_DIM] for a in heads], axis=0)
        sink = jnp.concatenate([jnp.full((blk_q, 1), sink_ref[a], F32) for a in heads], axis=0)

        def scores(kv_ref):
            return lax.dot_general(q, kv_ref[:, kc], nt, preferred_element_type=F32) * scale

        s_prev = jnp.where(valid_prev, scores(kvp_ref), NEG_INF)
        s_cur = scores(kvc_ref)
        s_next = jnp.where(valid_next, scores(kvn_ref), NEG_INF)
        s_ctx = scores(kvx_ref)
        m = jnp.maximum(jnp.maximum(jnp.max(s_prev, axis=-1, keepdims=True),
                                    jnp.max(s_cur, axis=-1, keepdims=True)),
                        jnp.maximum(jnp.max(s_next, axis=-1, keepdims=True),
                                    jnp.max(s_ctx, axis=-1, keepdims=True)))
        m = jnp.maximum(m, sink)
        e_prev, e_cur, e_next, e_ctx = (jnp.exp(s - m) for s in (s_prev, s_cur, s_next, s_ctx))
        denom = (jnp.sum(e_prev, axis=-1, keepdims=True) + jnp.sum(e_cur, axis=-1, keepdims=True)
                 + jnp.sum(e_next, axis=-1, keepdims=True) + jnp.sum(e_ctx, axis=-1, keepdims=True)
                 + jnp.exp(sink - m))
        inv = 1.0 / denom
        out = (_dot((e_prev * inv).astype(BF16), kvp_ref[:, vc])
               + _dot((e_cur * inv).astype(BF16), kvc_ref[:, vc])
               + _dot((e_next * inv).astype(BF16), kvn_ref[:, vc])
               + _dot((e_ctx * inv).astype(BF16), kvx_ref[:, vc]))
        for r, a in enumerate(heads):
            o_ref[:, a * HEAD_DIM:(a + 1) * HEAD_DIM] = out[r * blk_q:(r + 1) * blk_q, :].astype(BF16)


def _attn(q, kv, kv_ctx, sink, bsz, seq, c_len):
    n_tok = q.shape[0]
    nb = seq // ATTN_BLOCK
    kv_w = 2 * KV_WIDTH
    return pl.pallas_call(
        _attn_kernel,
        grid=(bsz, nb),
        in_specs=[pl.BlockSpec(memory_space=pltpu.SMEM),
                  pl.BlockSpec((ATTN_BLOCK, ATTN_WIDTH), lambda b, k: (b * nb + k, 0)),
                  pl.BlockSpec((ATTN_BLOCK, kv_w), lambda b, k: (b * nb + jnp.maximum(k - 1, 0), 0)),
                  pl.BlockSpec((ATTN_BLOCK, kv_w), lambda b, k: (b * nb + k, 0)),
                  pl.BlockSpec((ATTN_BLOCK, kv_w), lambda b, k: (b * nb + jnp.minimum(k + 1, nb - 1), 0)),
                  pl.BlockSpec((c_len, kv_w), lambda b, k: (b, 0))],
        out_specs=pl.BlockSpec((ATTN_BLOCK, ATTN_WIDTH), lambda b, k: (b * nb + k, 0)),
        out_shape=jax.ShapeDtypeStruct((n_tok, ATTN_WIDTH), BF16),
        compiler_params=_params("parallel", "arbitrary"),
        name="attn",
    )(sink, q, kv, kv, kv, kv_ctx)


def _merge_kernel(x_ref, mod_ref, g_ref, a_ref, b_ref, ga_ref, gb_ref, wa_ref, wb_ref, wo_ref, o_ref):
    merged = (ga_ref[...].astype(F32) * _dot(a_ref[...], wa_ref[...])
              + gb_ref[...].astype(F32) * _dot(b_ref[...], wb_ref[...]))
    y = _dot(merged.astype(BF16), wo_ref[...])
    o_ref[...] = x_ref[...] + mod_ref[0, 5:6, :] * (_rms(y) * g_ref[3:4, :])


def _merge(x, mod3, norm_g, mix_a, mix_b, gates, w_a, w_b, w_o, seq, tm):
    n_tok, d = x.shape
    tiles_per_seq = seq // tm
    const = lambda i: (0, 0)
    return pl.pallas_call(
        _merge_kernel,
        grid=(n_tok // tm,),
        in_specs=[pl.BlockSpec((tm, d), lambda i: (i, 0)),
                  pl.BlockSpec((1, N_MOD, d), lambda i: (i // tiles_per_seq, 0, 0)),
                  pl.BlockSpec(norm_g.shape, const),
                  pl.BlockSpec((tm, GMLP_WIDTH), lambda i: (i, 0)),
                  pl.BlockSpec((tm, ATTN_WIDTH), lambda i: (i, 0)),
                  pl.BlockSpec((tm, d), lambda i: (i, 0)),
                  pl.BlockSpec((tm, d), lambda i: (i, 1)),
                  pl.BlockSpec(w_a.shape, const),
                  pl.BlockSpec(w_b.shape, const),
                  pl.BlockSpec(w_o.shape, const)],
        out_specs=pl.BlockSpec((tm, d), lambda i: (i, 0)),
        out_shape=jax.ShapeDtypeStruct((n_tok, d), F32),
        compiler_params=_params("parallel"),
        name="merge",
    )(x, mod3, norm_g, mix_a, mix_b, gates, gates, w_a, w_b, w_o)


def _rope_tables(seq):
    rows = seq // GRID_W
    row = jnp.repeat(jnp.arange(rows), GRID_W).astype(F32)
    col = jnp.tile(jnp.arange(GRID_W), rows).astype(F32)
    quarter = HEAD_DIM // 4
    freqs = ROPE_BASE ** (-jnp.arange(quarter, dtype=F32) / quarter)
    ang_r = row[:, None] * freqs[None, :]
    ang_c = col[:, None] * freqs[None, :]
    ang = jnp.concatenate([ang_r, ang_r, ang_c, ang_c], axis=-1)
    cos, sin = jnp.cos(ang), jnp.sin(ang)
    first = (jnp.arange(HEAD_DIM) // quarter) % 2 == 0
    return cos, jnp.where(first, -sin, 0.0), jnp.where(first, 0.0, sin)


def kernel(x, c, ctx, c_ctx, w_ada, b_ada, norm_g, w_ffn_in, w_ffn_out, w_in, gmlp_ln_g, gmlp_ln_b,
           w_spatial, b_spatial, attn_sink, w_branch_a, w_branch_b, w_out):
    bsz, seq, d = x.shape
    c_len = ctx.shape[1]
    depth = w_ada.shape[0]
    ctx_row = bsz
    assert bsz < ADA_ROWS and seq % 512 == 0 and (bsz * c_len) % 256 == 0

    cos, sin_lo, sin_hi = _rope_tables(seq)
    c_all = jnp.zeros((ADA_ROWS, d), F32).at[:bsz].set(c).at[ctx_row].set(c_ctx)
    xt = x.reshape(bsz * seq, d)
    ct = ctx.reshape(bsz * c_len, d)
    tm_ffn = 512
    tm_ctx = min(512, bsz * c_len)
    x_row = lambda i: i // (seq // tm_ffn)
    c_row = lambda i: ctx_row

    for l in range(depth):
        last = l == depth - 1
        g = norm_g[l]
        wf_in = w_ffn_in[l].astype(BF16)
        wf_out = w_ffn_out[l].astype(BF16)
        w_in_l = w_in[l].astype(BF16)
        w_s = w_spatial[l].astype(BF16)
        b_full = jnp.repeat(b_spatial[l].T, GMLP_GROUP_DIM, axis=1)

        mod3 = _ada(c_all, w_ada[l], b_ada[l], d // 2).reshape(ADA_ROWS, N_MOD, d)

        xt = _ffn(xt, mod3, g, wf_in[0], wf_out[0], 0, x_row, tm_ffn, 512)
        ct = _ffn(ct, mod3, g, wf_in[0], wf_out[0], 0, c_row, tm_ctx, 512)

        q, kv, ug, gates = _proj(xt, mod3, g, cos, sin_lo, sin_hi, w_in_l, seq, 512, 512)
        kv_ctx = _ctx_kv(ct, mod3, g, w_in_l, ctx_row, min(256, bsz * c_len))
        mix_a = _gmlp(ug, gmlp_ln_g[l], gmlp_ln_b[l], w_s, b_full, 512)
        mix_b = _attn(q, kv, kv_ctx, attn_sink[l], bsz, seq, c_len)
        xt = _merge(xt, mod3, g, mix_a, mix_b, gates, w_branch_a[l].astype(BF16),
                    w_branch_b[l].astype(BF16), w_out[l].astype(BF16), seq, 256)

        xt = _ffn(xt, mod3, g, wf_in[1], wf_out[1], 2, x_row, tm_ffn, 512)
        assert last, "context-stream outputs of a non-final layer are not implemented"
    return xt.reshape(bsz, seq, d)
```

```python
import functools

import jax
import jax.numpy as jnp
from jax import lax
from jax.experimental import pallas as pl
from jax.experimental.pallas import tpu as pltpu

GRID_W = 64
HEAD_DIM = 128
N_HEADS = 8
N_KV_HEADS = 2
HEADS_PER_KV = N_HEADS // N_KV_HEADS
ATTN_WIDTH = N_HEADS * HEAD_DIM
KV_WIDTH = N_KV_HEADS * HEAD_DIM
WINDOW = 128
ATTN_BLOCK = 128
ROPE_BASE = 10000.0
CHUNK = 128
N_GMLP_GROUPS = 8
GMLP_WIDTH = 1024
GMLP_GROUP_DIM = GMLP_WIDTH // N_GMLP_GROUPS
FFN_RES_WEIGHT = 0.5
N_MOD = 9
NORM_EPS = 1e-6
NEG_INF = -1e30

ADA_ROWS = 8
V7X_VMEM_LIMIT = 58 * 1024 * 1024
PROJ_COLS = 512
ROW_CHUNK = 16
ROW_CHUNK_UNROLL = 4
FFN_ACT_COLS = 256
ATTN_ROW_CHUNK = 32

F32 = jnp.float32
BF16 = jnp.bfloat16


def _params(*sem):
    return pltpu.CompilerParams(dimension_semantics=sem, vmem_limit_bytes=V7X_VMEM_LIMIT)


def _dot(a, b):
    return jnp.dot(a, b, preferred_element_type=F32)


def _rms(t):
    return t * lax.rsqrt(jnp.mean(t * t, axis=-1, keepdims=True) + NORM_EPS)


def _for_row_chunks(n_rows, body):
    def step(k, carry):
        body(pl.ds(pl.multiple_of(k * ROW_CHUNK, ROW_CHUNK), ROW_CHUNK))
        return carry
    lax.fori_loop(0, n_rows // ROW_CHUNK, step, 0, unroll=ROW_CHUNK_UNROLL)


def _ada_kernel(c_ref, w_ref, b_ref, o_ref):
    c = c_ref[...]
    a = (c * jax.nn.sigmoid(c)).astype(BF16)
    o_ref[...] = _dot(a, w_ref[...].astype(BF16)) + b_ref[...]


def _ada(c_all, w_ada, b_ada, lyr, tn):
    _, d, n = w_ada.shape
    assert n % tn == 0
    return pl.pallas_call(
        _ada_kernel,
        grid=(n // tn,),
        in_specs=[pl.BlockSpec((ADA_ROWS, d), lambda j: (0, 0)),
                  pl.BlockSpec((None, d, tn), lambda j: (lyr, 0, j)),
                  pl.BlockSpec((1, tn), lambda j: (lyr, j))],
        out_specs=pl.BlockSpec((ADA_ROWS, tn), lambda j: (0, j)),
        out_shape=jax.ShapeDtypeStruct((ADA_ROWS, n), F32),
        compiler_params=_params("arbitrary"),
        name="ada",
    )(c_all, w_ada, b_ada)


def _ffn_kernel(sub, t_ref, mod_ref, g_ref, wg_ref, wu_ref, wo_ref, o_ref, h_ref, acc_ref):
    j = pl.program_id(1)
    tm = t_ref.shape[0]

    @pl.when(j == 0)
    def _():
        def body(rows):
            y = _rms(t_ref[rows, :]) * g_ref[2 * sub:2 * sub + 1, :]
            h = y * (1.0 + mod_ref[0, 3 * sub + 1:3 * sub + 2, :]) + mod_ref[0, 3 * sub:3 * sub + 1, :]
            h_ref[rows, :] = h.astype(BF16)
            acc_ref[rows, :] = jnp.zeros((ROW_CHUNK, acc_ref.shape[1]), F32)
        _for_row_chunks(tm, body)

    h = h_ref[...]
    tf = wg_ref.shape[1]
    acts = []
    for c0 in range(0, tf, FFN_ACT_COLS):
        gt = _dot(h, wg_ref[:, c0:c0 + FFN_ACT_COLS])
        up = _dot(h, wu_ref[:, c0:c0 + FFN_ACT_COLS])
        acts.append(((gt * jax.nn.sigmoid(gt)) * up).astype(BF16))
    acc_ref[...] += _dot(jnp.concatenate(acts, axis=1), wo_ref[...])

    @pl.when(j == pl.num_programs(1) - 1)
    def _():
        def body(rows):
            y = _rms(acc_ref[rows, :]) * g_ref[2 * sub + 1:2 * sub + 2, :]
            o_ref[rows, :] = t_ref[rows, :] + (FFN_RES_WEIGHT * mod_ref[0, 3 * sub + 2:3 * sub + 3, :]) * y
        _for_row_chunks(tm, body)


def _ffn(t, mod3, norm_g, w_in, w_out, lyr, sub, mod_row, tm, tf):
    n_tok, d = t.shape
    f = w_out.shape[2]
    nf = f // tf
    half = sub // 2
    return pl.pallas_call(
        functools.partial(_ffn_kernel, sub),
        grid=(n_tok // tm, nf),
        in_specs=[pl.BlockSpec((tm, d), lambda i, j: (i, 0)),
                  pl.BlockSpec((1, N_MOD, d), lambda i, j: (mod_row(i), 0, 0)),
                  pl.BlockSpec(norm_g.shape, lambda i, j: (0, 0)),
                  pl.BlockSpec((None, None, d, tf), lambda i, j: (lyr, half, 0, j)),
                  pl.BlockSpec((None, None, d, tf), lambda i, j: (lyr, half, 0, nf + j)),
                  pl.BlockSpec((None, None, tf, d), lambda i, j: (lyr, half, j, 0))],
        out_specs=pl.BlockSpec((tm, d), lambda i, j: (i, 0)),
        out_shape=jax.ShapeDtypeStruct((n_tok, d), F32),
        scratch_shapes=[pltpu.VMEM((tm, d), BF16), pltpu.VMEM((tm, d), F32)],
        compiler_params=_params("parallel", "arbitrary"),
        name=f"ffn{sub}",
    )(t, mod3, norm_g, w_in, w_in, w_out)


def _rope(x, cos, sin_lo, sin_hi):
    return (x * cos + pltpu.roll(x, HEAD_DIM - HEAD_DIM // 4, 1) * sin_lo
            + pltpu.roll(x, HEAD_DIM // 4, 1) * sin_hi)


def _modulated(t_ref, mod_ref, g_ref):
    y = _rms(t_ref[...]) * g_ref[2:3, :]
    return y * (1.0 + mod_ref[0, 4:5, :]) + mod_ref[0, 3:4, :]


def _proj_kernel(t_ref, mod_ref, g_ref, cos_ref, slo_ref, shi_ref, w_ref,
                 q_ref, kv_ref, ug_ref, gate_ref, h_ref):
    h_ref[...] = _modulated(t_ref, mod_ref, g_ref).astype(BF16)
    tn = PROJ_COLS

    def cols(c):
        return _dot(h_ref[...], w_ref[:, c * tn:(c + 1) * tn])

    def rope_into(o_ref, p, col0, n_heads):
        cos, slo, shi = cos_ref[...], slo_ref[...], shi_ref[...]
        for a in range(n_heads):
            x = p[:, a * HEAD_DIM:(a + 1) * HEAD_DIM]
            o_ref[:, col0 + a * HEAD_DIM:col0 + (a + 1) * HEAD_DIM] = _rope(x, cos, slo, shi).astype(BF16)

    c = 0
    for k in range(ATTN_WIDTH // tn):
        rope_into(q_ref, cols(c), k * tn, tn // HEAD_DIM)
        c += 1
    p = cols(c)
    rope_into(kv_ref, p, 0, N_KV_HEADS)
    kv_ref[:, KV_WIDTH:] = p[:, KV_WIDTH:].astype(BF16)
    c += 1
    for k in range(ug_ref.shape[1] // tn):
        ug_ref[:, k * tn:(k + 1) * tn] = jax.nn.gelu(cols(c)).astype(BF16)
        c += 1
    for k in range(gate_ref.shape[1] // tn):
        gate_ref[:, k * tn:(k + 1) * tn] = jax.nn.sigmoid(cols(c)).astype(BF16)
        c += 1


def _proj(x, mod3, norm_g, cos, sin_lo, sin_hi, w_in, lyr, seq, tm):
    n_tok, d = x.shape
    p_width = w_in.shape[2]
    assert PROJ_COLS == 2 * KV_WIDTH and ATTN_WIDTH % PROJ_COLS == 0 and d % PROJ_COLS == 0
    assert p_width == ATTN_WIDTH + 2 * KV_WIDTH + 2 * GMLP_WIDTH + 2 * d
    tiles_per_seq = seq // tm
    hd_spec = pl.BlockSpec((tm, HEAD_DIM), lambda i: (i % tiles_per_seq, 0))
    widths = (ATTN_WIDTH, 2 * KV_WIDTH, 2 * GMLP_WIDTH, 2 * d)
    return pl.pallas_call(
        _proj_kernel,
        grid=(n_tok // tm,),
        in_specs=[pl.BlockSpec((tm, d), lambda i: (i, 0)),
                  pl.BlockSpec((1, N_MOD, d), lambda i: (i // tiles_per_seq, 0, 0)),
                  pl.BlockSpec(norm_g.shape, lambda i: (0, 0)),
                  hd_spec, hd_spec, hd_spec,
                  pl.BlockSpec((None, d, p_width), lambda i: (lyr, 0, 0), pipeline_mode=pl.Buffered(1))],
        out_specs=[pl.BlockSpec((tm, w), lambda i: (i, 0)) for w in widths],
        out_shape=[jax.ShapeDtypeStruct((n_tok, w), BF16) for w in widths],
        scratch_shapes=[pltpu.VMEM((tm, d), BF16)],
        compiler_params=_params("parallel"),
        name="proj",
    )(x, mod3, norm_g, cos, sin_lo, sin_hi, w_in)


def _ctx_kv_kernel(t_ref, mod_ref, g_ref, w_ref, kv_ref):
    kv_ref[...] = _dot(_modulated(t_ref, mod_ref, g_ref).astype(BF16), w_ref[...]).astype(BF16)


def _ctx_kv(ctx, mod3, norm_g, w_in, lyr, ctx_row, tm):
    n_tok, d = ctx.shape
    tn = 2 * KV_WIDTH
    return pl.pallas_call(
        _ctx_kv_kernel,
        grid=(n_tok // tm,),
        in_specs=[pl.BlockSpec((tm, d), lambda i: (i, 0)),
                  pl.BlockSpec((1, N_MOD, d), lambda i: (ctx_row, 0, 0)),
                  pl.BlockSpec(norm_g.shape, lambda i: (0, 0)),
                  pl.BlockSpec((None, d, tn), lambda i: (lyr, 0, ATTN_WIDTH // tn))],
        out_specs=pl.BlockSpec((tm, tn), lambda i: (i, 0)),
        out_shape=jax.ShapeDtypeStruct((n_tok, tn), BF16),
        compiler_params=_params("parallel"),
        name="ctx_kv",
    )(ctx, mod3, norm_g, w_in)


def _gmlp_kernel(u_ref, v_ref, lg_ref, lb_ref, ws_ref, bs_ref, o_ref):
    tm = u_ref.shape[0]
    for c in range(tm // CHUNK):
        rows = slice(c * CHUNK, (c + 1) * CHUNK)
        v = v_ref[rows, :].astype(F32)
        mu = jnp.mean(v, axis=-1, keepdims=True)
        vc = v - mu
        var = jnp.mean(vc * vc, axis=-1, keepdims=True)
        vn = ((vc * lax.rsqrt(var + NORM_EPS)) * lg_ref[...] + lb_ref[...]).astype(BF16)
        for g in range(N_GMLP_GROUPS):
            cols = slice(g * GMLP_GROUP_DIM, (g + 1) * GMLP_GROUP_DIM)
            mixed = _dot(ws_ref[g], vn[:, cols]) + bs_ref[:, cols]
            o_ref[rows, cols] = (u_ref[rows, cols].astype(F32) * mixed).astype(BF16)


def _gmlp(ug, ln_g, ln_b, w_s, b_full, lyr, tm):
    n_tok = ug.shape[0]
    return pl.pallas_call(
        _gmlp_kernel,
        grid=(n_tok // tm,),
        in_specs=[pl.BlockSpec((tm, GMLP_WIDTH), lambda i: (i, 0)),
                  pl.BlockSpec((tm, GMLP_WIDTH), lambda i: (i, 1)),
                  pl.BlockSpec((1, GMLP_WIDTH), lambda i: (lyr, 0)),
                  pl.BlockSpec((1, GMLP_WIDTH), lambda i: (lyr, 0)),
                  pl.BlockSpec((None,) + w_s.shape[1:], lambda i: (lyr, 0, 0, 0)),
                  pl.BlockSpec(b_full.shape, lambda i: (0, 0))],
        out_specs=pl.BlockSpec((tm, GMLP_WIDTH), lambda i: (i, 0)),
        out_shape=jax.ShapeDtypeStruct((n_tok, GMLP_WIDTH), BF16),
        compiler_params=_params("parallel"),
        name="gmlp",
    )(ug, ug, ln_g, ln_b, w_s, b_full)


def _attn_kernel(sink_ref, q_ref, kvp_ref, kvc_ref, kvn_ref, kvx_ref, bprev_ref, bnext_ref, o_ref, s_ref, p_ref):
    blk_q = q_ref.shape[0]
    rows = HEADS_PER_KV * blk_q
    rc = ATTN_ROW_CHUNK
    kv_refs = (kvp_ref, kvc_ref, kvn_ref, kvx_ref)
    log2e = 1.4426950408889634
    scale2 = HEAD_DIM ** -0.5 * log2e
    nt = (((1,), (1,)), ((), ()))
    n_pieces = s_ref.shape[2] // ATTN_BLOCK
    group_heads = [range(g * HEADS_PER_KV, (g + 1) * HEADS_PER_KV) for g in range(N_KV_HEADS)]

    for g, heads in enumerate(group_heads):
        kc = slice(g * HEAD_DIM, (g + 1) * HEAD_DIM)
        q = jnp.concatenate([q_ref[:, a * HEAD_DIM:(a + 1) * HEAD_DIM] for a in heads], axis=0)
        k_all = jnp.concatenate([r[:, kc] for r in kv_refs], axis=0)
        s_ref[g] = lax.dot_general(q, k_all, nt, preferred_element_type=F32)

    for g, heads in enumerate(group_heads):
        invs = []
        for r in range(0, rows, rc):
            sink2 = sink_ref[heads[r // blk_q]] * log2e
            rq = slice(r % blk_q, r % blk_q + rc)
            pieces = [s_ref[g, r:r + rc, n * ATTN_BLOCK:(n + 1) * ATTN_BLOCK] for n in range(n_pieces)]
            pieces[0] = pieces[0] + bprev_ref[rq, :]
            pieces[2] = pieces[2] + bnext_ref[rq, :]
            m = jnp.max(functools.reduce(jnp.maximum, pieces), axis=-1, keepdims=True)
            m2 = jnp.maximum(m * scale2, sink2)
            es = [jnp.exp2(p * scale2 - m2) for p in pieces]
            denom = jnp.sum(functools.reduce(jnp.add, es), axis=-1, keepdims=True) + jnp.exp2(sink2 - m2)
            invs.append(1.0 / denom)
            for n, e in enumerate(es):
                p_ref[g, r:r + rc, n * ATTN_BLOCK:(n + 1) * ATTN_BLOCK] = e.astype(BF16)
        vc = slice(KV_WIDTH + g * HEAD_DIM, KV_WIDTH + (g + 1) * HEAD_DIM)
        v_all = jnp.concatenate([r[:, vc] for r in kv_refs], axis=0)
        out = _dot(p_ref[g], v_all) * jnp.concatenate(invs, axis=0)
        for r, a in enumerate(heads):
            o_ref[:, a * HEAD_DIM:(a + 1) * HEAD_DIM] = out[r * blk_q:(r + 1) * blk_q, :].astype(BF16)


def _band_bias():
    qi = jnp.arange(ATTN_BLOCK)[:, None]
    kj = jnp.arange(ATTN_BLOCK)[None, :]
    neg = jnp.full((ATTN_BLOCK, ATTN_BLOCK), NEG_INF, F32)
    prev = jnp.where(kj >= qi, 0.0, NEG_INF).astype(F32)
    nxt = jnp.where(kj <= qi, 0.0, NEG_INF).astype(F32)
    return jnp.stack([jnp.stack([prev, neg]), jnp.stack([nxt, neg])])


def _attn(q, kv, kv_ctx, sink, lyr, bsz, seq, c_len):
    n_tok = q.shape[0]
    nb = seq // ATTN_BLOCK
    kv_w = 2 * KV_WIDTH
    assert c_len % ATTN_BLOCK == 0 and WINDOW == ATTN_BLOCK
    n_keys = 3 * ATTN_BLOCK + c_len
    rows = HEADS_PER_KV * ATTN_BLOCK
    bias = _band_bias()
    bias_block = (None, None, ATTN_BLOCK, ATTN_BLOCK)
    return pl.pallas_call(
        _attn_kernel,
        grid=(bsz, nb),
        in_specs=[pl.BlockSpec(memory_space=pltpu.SMEM),
                  pl.BlockSpec((ATTN_BLOCK, ATTN_WIDTH), lambda b, k: (b * nb + k, 0)),
                  pl.BlockSpec((ATTN_BLOCK, kv_w), lambda b, k: (b * nb + jnp.maximum(k - 1, 0), 0)),
                  pl.BlockSpec((ATTN_BLOCK, kv_w), lambda b, k: (b * nb + k, 0)),
                  pl.BlockSpec((ATTN_BLOCK, kv_w), lambda b, k: (b * nb + jnp.minimum(k + 1, nb - 1), 0)),
                  pl.BlockSpec((c_len, kv_w), lambda b, k: (b, 0)),
                  pl.BlockSpec(bias_block, lambda b, k: (0, (k == 0).astype(jnp.int32), 0, 0)),
                  pl.BlockSpec(bias_block, lambda b, k: (1, (k == nb - 1).astype(jnp.int32), 0, 0))],
        out_specs=pl.BlockSpec((ATTN_BLOCK, ATTN_WIDTH), lambda b, k: (b * nb + k, 0)),
        out_shape=jax.ShapeDtypeStruct((n_tok, ATTN_WIDTH), BF16),
        scratch_shapes=[pltpu.VMEM((N_KV_HEADS, rows, n_keys), F32),
                        pltpu.VMEM((N_KV_HEADS, rows, n_keys), BF16)],
        compiler_params=_params("parallel", "arbitrary"),
        name="attn",
    )(sink[lyr], q, kv, kv, kv, kv_ctx, bias, bias)


def _merge_kernel(x_ref, mod_ref, g_ref, a_ref, b_ref, ga_ref, gb_ref, wa_ref, wb_ref, wo_ref, o_ref):
    merged = (ga_ref[...].astype(F32) * _dot(a_ref[...], wa_ref[...])
              + gb_ref[...].astype(F32) * _dot(b_ref[...], wb_ref[...]))
    y = _dot(merged.astype(BF16), wo_ref[...])
    o_ref[...] = x_ref[...] + mod_ref[0, 5:6, :] * (_rms(y) * g_ref[3:4, :])


def _merge(x, mod3, norm_g, mix_a, mix_b, gates, w_a, w_b, w_o, lyr, seq, tm):
    n_tok, d = x.shape
    tiles_per_seq = seq // tm
    weight = lambda w: pl.BlockSpec((None,) + w.shape[1:], lambda i: (lyr, 0, 0), pipeline_mode=pl.Buffered(1))
    return pl.pallas_call(
        _merge_kernel,
        grid=(n_tok // tm,),
        in_specs=[pl.BlockSpec((tm, d), lambda i: (i, 0)),
                  pl.BlockSpec((1, N_MOD, d), lambda i: (i // tiles_per_seq, 0, 0)),
                  pl.BlockSpec(norm_g.shape, lambda i: (0, 0)),
                  pl.BlockSpec((tm, GMLP_WIDTH), lambda i: (i, 0)),
                  pl.BlockSpec((tm, ATTN_WIDTH), lambda i: (i, 0)),
                  pl.BlockSpec((tm, d), lambda i: (i, 0)),
                  pl.BlockSpec((tm, d), lambda i: (i, 1)),
                  weight(w_a), weight(w_b), weight(w_o)],
        out_specs=pl.BlockSpec((tm, d), lambda i: (i, 0)),
        out_shape=jax.ShapeDtypeStruct((n_tok, d), F32),
        compiler_params=_params("parallel"),
        name="merge",
    )(x, mod3, norm_g, mix_a, mix_b, gates, gates, w_a, w_b, w_o)


def _rope_tables(seq):
    rows = seq // GRID_W
    row = jnp.repeat(jnp.arange(rows), GRID_W).astype(F32)
    col = jnp.tile(jnp.arange(GRID_W), rows).astype(F32)
    quarter = HEAD_DIM // 4
    freqs = ROPE_BASE ** (-jnp.arange(quarter, dtype=F32) / quarter)
    ang_r = row[:, None] * freqs[None, :]
    ang_c = col[:, None] * freqs[None, :]
    ang = jnp.concatenate([ang_r, ang_r, ang_c, ang_c], axis=-1)
    cos, sin = jnp.cos(ang), jnp.sin(ang)
    first = (jnp.arange(HEAD_DIM) // quarter) % 2 == 0
    return cos, jnp.where(first, -sin, 0.0), jnp.where(first, 0.0, sin)


def kernel(x, c, ctx, c_ctx, w_ada, b_ada, norm_g, w_ffn_in, w_ffn_out, w_in, gmlp_ln_g, gmlp_ln_b,
           w_spatial, b_spatial, attn_sink, w_branch_a, w_branch_b, w_out):
    bsz, seq, d = x.shape
    c_len = ctx.shape[1]
    depth = w_ada.shape[0]
    ctx_row = bsz
    assert bsz < ADA_ROWS and seq % 512 == 0 and (bsz * c_len) % 256 == 0
    assert depth == 1, "context-stream outputs of a non-final layer are not implemented"

    cos, sin_lo, sin_hi = _rope_tables(seq)
    c_all = jnp.zeros((ADA_ROWS, d), F32).at[:bsz].set(c).at[ctx_row].set(c_ctx)
    xt = x.reshape(bsz * seq, d)
    ct = ctx.reshape(bsz * c_len, d)
    tm_ffn = 1024
    tf_ffn = 256
    tm_ctx = min(1024, bsz * c_len)
    assert seq % tm_ffn == 0 and (bsz * c_len) % tm_ctx == 0
    x_row = lambda i: i // (seq // tm_ffn)
    c_row = lambda i: ctx_row

    wf_in = w_ffn_in.astype(BF16)
    wf_out = w_ffn_out.astype(BF16)
    w_in_b = w_in.astype(BF16)
    w_s = w_spatial.astype(BF16)
    w_a, w_b, w_o = w_branch_a.astype(BF16), w_branch_b.astype(BF16), w_out.astype(BF16)

    for l in range(depth):
        g = norm_g[l]
        b_full = jnp.repeat(b_spatial[l].T, GMLP_GROUP_DIM, axis=1)
        mod3 = _ada(c_all, w_ada, b_ada, l, d // 2).reshape(ADA_ROWS, N_MOD, d)

        xt = _ffn(xt, mod3, g, wf_in, wf_out, l, 0, x_row, tm_ffn, tf_ffn)
        ct = _ffn(ct, mod3, g, wf_in, wf_out, l, 0, c_row, tm_ctx, tf_ffn)

        q, kv, ug, gates = _proj(xt, mod3, g, cos, sin_lo, sin_hi, w_in_b, l, seq, 256)
        kv_ctx = _ctx_kv(ct, mod3, g, w_in_b, l, ctx_row, min(256, bsz * c_len))
        mix_a = _gmlp(ug, gmlp_ln_g, gmlp_ln_b, w_s, b_full, l, 512)
        mix_b = _attn(q, kv, kv_ctx, attn_sink, l, bsz, seq, c_len)
        xt = _merge(xt, mod3, g, mix_a, mix_b, gates, w_a, w_b, w_o, l, seq, 256)

        xt = _ffn(xt, mod3, g, wf_in, wf_out, l, 2, x_row, tm_ffn, tf_ffn)
    return xt.reshape(bsz, seq, d)
```

```python
import functools

import jax
import jax.numpy as jnp
from jax import lax
from jax.experimental import pallas as pl
from jax.experimental.pallas import tpu as pltpu

GRID_W = 64
HEAD_DIM = 128
N_HEADS = 8
N_KV_HEADS = 2
HEADS_PER_KV = N_HEADS // N_KV_HEADS
ATTN_WIDTH = N_HEADS * HEAD_DIM
KV_WIDTH = N_KV_HEADS * HEAD_DIM
WINDOW = 128
ATTN_BLOCK = 128
ROPE_BASE = 10000.0
CHUNK = 128
N_GMLP_GROUPS = 8
GMLP_WIDTH = 1024
GMLP_GROUP_DIM = GMLP_WIDTH // N_GMLP_GROUPS
FFN_RES_WEIGHT = 0.5
N_MOD = 9
NORM_EPS = 1e-6
NEG_INF = -1e30

ADA_ROWS = 8
V7X_VMEM_LIMIT = 58 * 1024 * 1024
PROJ_COLS = 512
SUBLANES = 8
ROW_CHUNK = 16
ROW_CHUNK_UNROLL = 8
STATS_UNROLL = 16
FFN_ACT_COLS = 256
ATTN_ROW_CHUNK = 32

F32 = jnp.float32
BF16 = jnp.bfloat16


def _params(*sem):
    return pltpu.CompilerParams(dimension_semantics=sem, vmem_limit_bytes=V7X_VMEM_LIMIT)


def _dot(a, b):
    return jnp.dot(a, b, preferred_element_type=F32)


def _rms(t):
    return t * lax.rsqrt(jnp.mean(t * t, axis=-1, keepdims=True) + NORM_EPS)


def _chunk_rows(v8):
    return jnp.concatenate([v8] * (ROW_CHUNK // SUBLANES), axis=0)


def _for_row_chunks(n_rows, body, unroll=ROW_CHUNK_UNROLL):
    def step(k, carry):
        body(pl.ds(pl.multiple_of(k * ROW_CHUNK, ROW_CHUNK), ROW_CHUNK))
        return carry
    lax.fori_loop(0, n_rows // ROW_CHUNK, step, 0, unroll=unroll)


def _ada_kernel(c_ref, w_ref, b_ref, o_ref):
    c = c_ref[...]
    a = (c * jax.nn.sigmoid(c)).astype(BF16)
    o_ref[...] = _dot(a, w_ref[...].astype(BF16)) + b_ref[...]


def _ada(c_all, w_ada, b_ada, lyr, tn):
    _, d, n = w_ada.shape
    assert n % tn == 0
    return pl.pallas_call(
        _ada_kernel,
        grid=(n // tn,),
        in_specs=[pl.BlockSpec((ADA_ROWS, d), lambda j: (0, 0)),
                  pl.BlockSpec((None, d, tn), lambda j: (lyr, 0, j)),
                  pl.BlockSpec((1, tn), lambda j: (lyr, j))],
        out_specs=pl.BlockSpec((ADA_ROWS, tn), lambda j: (0, j)),
        out_shape=jax.ShapeDtypeStruct((ADA_ROWS, n), F32),
        compiler_params=_params("arbitrary"),
        name="ada",
    )(c_all, w_ada, b_ada)


def _ffn_kernel(sub, t_ref, mod_ref, g_ref, wg_ref, wu_ref, wo_ref, o_ref, h_ref, rstd_ref):
    j = pl.program_id(1)
    tm = t_ref.shape[0]

    @pl.when(j == 0)
    def _():
        def body(rows):
            y = _rms(t_ref[rows, :]) * _chunk_rows(g_ref[2 * sub])
            h = y * _chunk_rows(1.0 + mod_ref[0, 3 * sub + 1]) + _chunk_rows(mod_ref[0, 3 * sub])
            h_ref[rows, :] = h.astype(BF16)
            o_ref[rows, :] = jnp.zeros((ROW_CHUNK, o_ref.shape[1]), F32)
        _for_row_chunks(tm, body)

    h = h_ref[...]
    tf = wg_ref.shape[1]
    acts = []
    for c0 in range(0, tf, FFN_ACT_COLS):
        gt = _dot(h, wg_ref[:, c0:c0 + FFN_ACT_COLS])
        up = _dot(h, wu_ref[:, c0:c0 + FFN_ACT_COLS])
        acts.append(((gt * jax.nn.sigmoid(gt)) * up).astype(BF16))
    o_ref[...] += _dot(jnp.concatenate(acts, axis=1), wo_ref[...])

    @pl.when(j == pl.num_programs(1) - 1)
    def _():
        def stats(rows):
            a = o_ref[rows, :]
            rstd_ref[rows, :] = lax.rsqrt(jnp.mean(a * a, axis=-1, keepdims=True) + NORM_EPS)
        _for_row_chunks(tm, stats, unroll=STATS_UNROLL)

        def finish(rows):
            y = (o_ref[rows, :] * rstd_ref[rows, :]) * _chunk_rows(g_ref[2 * sub + 1])
            o_ref[rows, :] = t_ref[rows, :] + _chunk_rows(FFN_RES_WEIGHT * mod_ref[0, 3 * sub + 2]) * y
        _for_row_chunks(tm, finish)


def _ffn(t, mod8, g8, w_in, w_out, lyr, sub, mod_row, tm, tf):
    n_tok, d = t.shape
    f = w_out.shape[2]
    nf = f // tf
    half = sub // 2
    return pl.pallas_call(
        functools.partial(_ffn_kernel, sub),
        grid=(n_tok // tm, nf),
        in_specs=[pl.BlockSpec((tm, d), lambda i, j: (i, 0)),
                  pl.BlockSpec((1, N_MOD, SUBLANES, d), lambda i, j: (mod_row(i), 0, 0, 0)),
                  pl.BlockSpec(g8.shape, lambda i, j: (0, 0, 0)),
                  pl.BlockSpec((None, None, d, tf), lambda i, j: (lyr, half, 0, j)),
                  pl.BlockSpec((None, None, d, tf), lambda i, j: (lyr, half, 0, nf + j)),
                  pl.BlockSpec((None, None, tf, d), lambda i, j: (lyr, half, j, 0))],
        out_specs=pl.BlockSpec((tm, d), lambda i, j: (i, 0)),
        out_shape=jax.ShapeDtypeStruct((n_tok, d), F32),
        scratch_shapes=[pltpu.VMEM((tm, d), BF16), pltpu.VMEM((tm, 1), F32)],
        compiler_params=_params("parallel", "arbitrary"),
        name=f"ffn{sub}",
    )(t, mod8, g8, w_in, w_in, w_out)


def _rope(x, cos, sin_lo, sin_hi):
    return (x * cos + pltpu.roll(x, HEAD_DIM - HEAD_DIM // 4, 1) * sin_lo
            + pltpu.roll(x, HEAD_DIM // 4, 1) * sin_hi)


def _modulated(t_ref, mod_ref, g_ref):
    y = _rms(t_ref[...]) * g_ref[2:3, :]
    return y * (1.0 + mod_ref[0, 4:5, :]) + mod_ref[0, 3:4, :]


def _proj_kernel(t_ref, mod_ref, g_ref, cos_ref, slo_ref, shi_ref, w_ref,
                 q_ref, kv_ref, ug_ref, gate_ref, h_ref):
    h_ref[...] = _modulated(t_ref, mod_ref, g_ref).astype(BF16)
    tn = PROJ_COLS

    def cols(c):
        return _dot(h_ref[...], w_ref[:, c * tn:(c + 1) * tn])

    def rope_into(o_ref, p, col0, n_heads):
        cos, slo, shi = cos_ref[...], slo_ref[...], shi_ref[...]
        for a in range(n_heads):
            x = p[:, a * HEAD_DIM:(a + 1) * HEAD_DIM]
            o_ref[:, col0 + a * HEAD_DIM:col0 + (a + 1) * HEAD_DIM] = _rope(x, cos, slo, shi).astype(BF16)

    c = 0
    for k in range(ATTN_WIDTH // tn):
        rope_into(q_ref, cols(c), k * tn, tn // HEAD_DIM)
        c += 1
    p = cols(c)
    rope_into(kv_ref, p, 0, N_KV_HEADS)
    kv_ref[:, KV_WIDTH:] = p[:, KV_WIDTH:].astype(BF16)
    c += 1
    for k in range(ug_ref.shape[1] // tn):
        ug_ref[:, k * tn:(k + 1) * tn] = jax.nn.gelu(cols(c)).astype(BF16)
        c += 1
    for k in range(gate_ref.shape[1] // tn):
        gate_ref[:, k * tn:(k + 1) * tn] = jax.nn.sigmoid(cols(c)).astype(BF16)
        c += 1


def _proj(x, mod3, norm_g, cos, sin_lo, sin_hi, w_in, lyr, seq, tm):
    n_tok, d = x.shape
    p_width = w_in.shape[2]
    assert PROJ_COLS == 2 * KV_WIDTH and ATTN_WIDTH % PROJ_COLS == 0 and d % PROJ_COLS == 0
    assert p_width == ATTN_WIDTH + 2 * KV_WIDTH + 2 * GMLP_WIDTH + 2 * d
    tiles_per_seq = seq // tm
    hd_spec = pl.BlockSpec((tm, HEAD_DIM), lambda i: (i % tiles_per_seq, 0))
    widths = (ATTN_WIDTH, 2 * KV_WIDTH, 2 * GMLP_WIDTH, 2 * d)
    return pl.pallas_call(
        _proj_kernel,
        grid=(n_tok // tm,),
        in_specs=[pl.BlockSpec((tm, d), lambda i: (i, 0)),
                  pl.BlockSpec((1, N_MOD, d), lambda i: (i // tiles_per_seq, 0, 0)),
                  pl.BlockSpec(norm_g.shape, lambda i: (0, 0)),
                  hd_spec, hd_spec, hd_spec,
                  pl.BlockSpec((None, d, p_width), lambda i: (lyr, 0, 0), pipeline_mode=pl.Buffered(1))],
        out_specs=[pl.BlockSpec((tm, w), lambda i: (i, 0)) for w in widths],
        out_shape=[jax.ShapeDtypeStruct((n_tok, w), BF16) for w in widths],
        scratch_shapes=[pltpu.VMEM((tm, d), BF16)],
        compiler_params=_params("parallel"),
        name="proj",
    )(x, mod3, norm_g, cos, sin_lo, sin_hi, w_in)


def _ctx_kv_kernel(t_ref, mod_ref, g_ref, w_ref, kv_ref):
    kv_ref[...] = _dot(_modulated(t_ref, mod_ref, g_ref).astype(BF16), w_ref[...]).astype(BF16)


def _ctx_kv(ctx, mod3, norm_g, w_in, lyr, ctx_row, tm):
    n_tok, d = ctx.shape
    tn = 2 * KV_WIDTH
    return pl.pallas_call(
        _ctx_kv_kernel,
        grid=(n_tok // tm,),
        in_specs=[pl.BlockSpec((tm, d), lambda i: (i, 0)),
                  pl.BlockSpec((1, N_MOD, d), lambda i: (ctx_row, 0, 0)),
                  pl.BlockSpec(norm_g.shape, lambda i: (0, 0)),
                  pl.BlockSpec((None, d, tn), lambda i: (lyr, 0, ATTN_WIDTH // tn))],
        out_specs=pl.BlockSpec((tm, tn), lambda i: (i, 0)),
        out_shape=jax.ShapeDtypeStruct((n_tok, tn), BF16),
        compiler_params=_params("parallel"),
        name="ctx_kv",
    )(ctx, mod3, norm_g, w_in)


def _gmlp_kernel(u_ref, v_ref, lg_ref, lb_ref, ws_ref, bs_ref, o_ref):
    tm = u_ref.shape[0]
    for c in range(tm // CHUNK):
        rows = slice(c * CHUNK, (c + 1) * CHUNK)
        v = v_ref[rows, :].astype(F32)
        mu = jnp.mean(v, axis=-1, keepdims=True)
        vc = v - mu
        var = jnp.mean(vc * vc, axis=-1, keepdims=True)
        vn = ((vc * lax.rsqrt(var + NORM_EPS)) * lg_ref[...] + lb_ref[...]).astype(BF16)
        for g in range(N_GMLP_GROUPS):
            cols = slice(g * GMLP_GROUP_DIM, (g + 1) * GMLP_GROUP_DIM)
            mixed = _dot(ws_ref[g], vn[:, cols]) + bs_ref[:, cols]
            o_ref[rows, cols] = (u_ref[rows, cols].astype(F32) * mixed).astype(BF16)


def _gmlp(ug, ln_g, ln_b, w_s, b_full, lyr, tm):
    n_tok = ug.shape[0]
    return pl.pallas_call(
        _gmlp_kernel,
        grid=(n_tok // tm,),
        in_specs=[pl.BlockSpec((tm, GMLP_WIDTH), lambda i: (i, 0)),
                  pl.BlockSpec((tm, GMLP_WIDTH), lambda i: (i, 1)),
                  pl.BlockSpec((1, GMLP_WIDTH), lambda i: (lyr, 0)),
                  pl.BlockSpec((1, GMLP_WIDTH), lambda i: (lyr, 0)),
                  pl.BlockSpec((None,) + w_s.shape[1:], lambda i: (lyr, 0, 0, 0)),
                  pl.BlockSpec(b_full.shape, lambda i: (0, 0))],
        out_specs=pl.BlockSpec((tm, GMLP_WIDTH), lambda i: (i, 0)),
        out_shape=jax.ShapeDtypeStruct((n_tok, GMLP_WIDTH), BF16),
        compiler_params=_params("parallel"),
        name="gmlp",
    )(ug, ug, ln_g, ln_b, w_s, b_full)


def _attn_kernel(sink_ref, q_ref, kvp_ref, kvc_ref, kvn_ref, kvx_ref, bprev_ref, bnext_ref, o_ref, s_ref, p_ref):
    blk_q = q_ref.shape[0]
    rows = HEADS_PER_KV * blk_q
    rc = ATTN_ROW_CHUNK
    kv_refs = (kvp_ref, kvc_ref, kvn_ref, kvx_ref)
    log2e = 1.4426950408889634
    scale2 = HEAD_DIM ** -0.5 * log2e
    nt = (((1,), (1,)), ((), ()))
    n_pieces = s_ref.shape[2] // ATTN_BLOCK
    group_heads = [range(g * HEADS_PER_KV, (g + 1) * HEADS_PER_KV) for g in range(N_KV_HEADS)]

    for g, heads in enumerate(group_heads):
        kc = slice(g * HEAD_DIM, (g + 1) * HEAD_DIM)
        q = jnp.concatenate([q_ref[:, a * HEAD_DIM:(a + 1) * HEAD_DIM] for a in heads], axis=0)
        k_all = jnp.concatenate([r[:, kc] for r in kv_refs], axis=0)
        s_ref[g] = lax.dot_general(q, k_all, nt, preferred_element_type=F32)

    for g, heads in enumerate(group_heads):
        invs = []
        for r in range(0, rows, rc):
            sink2 = sink_ref[heads[r // blk_q]] * log2e
            rq = slice(r % blk_q, r % blk_q + rc)
            pieces = [s_ref[g, r:r + rc, n * ATTN_BLOCK:(n + 1) * ATTN_BLOCK] for n in range(n_pieces)]
            pieces[0] = pieces[0] + bprev_ref[rq, :]
            pieces[2] = pieces[2] + bnext_ref[rq, :]
            m = jnp.max(functools.reduce(jnp.maximum, pieces), axis=-1, keepdims=True)
            m2 = jnp.maximum(m * scale2, sink2)
            es = [jnp.exp2(p * scale2 - m2) for p in pieces]
            denom = jnp.sum(functools.reduce(jnp.add, es), axis=-1, keepdims=True) + jnp.exp2(sink2 - m2)
            invs.append(1.0 / denom)
            for n, e in enumerate(es):
                p_ref[g, r:r + rc, n * ATTN_BLOCK:(n + 1) * ATTN_BLOCK] = e.astype(BF16)
        vc = slice(KV_WIDTH + g * HEAD_DIM, KV_WIDTH + (g + 1) * HEAD_DIM)
        v_all = jnp.concatenate([r[:, vc] for r in kv_refs], axis=0)
        out = _dot(p_ref[g], v_all) * jnp.concatenate(invs, axis=0)
        for r, a in enumerate(heads):
            o_ref[:, a * HEAD_DIM:(a + 1) * HEAD_DIM] = out[r * blk_q:(r + 1) * blk_q, :].astype(BF16)


def _band_bias():
    qi = jnp.arange(ATTN_BLOCK)[:, None]
    kj = jnp.arange(ATTN_BLOCK)[None, :]
    neg = jnp.full((ATTN_BLOCK, ATTN_BLOCK), NEG_INF, F32)
    prev = jnp.where(kj >= qi, 0.0, NEG_INF).astype(F32)
    nxt = jnp.where(kj <= qi, 0.0, NEG_INF).astype(F32)
    return jnp.stack([jnp.stack([prev, neg]), jnp.stack([nxt, neg])])


def _attn(q, kv, kv_ctx, sink, lyr, bsz, seq, c_len):
    n_tok = q.shape[0]
    nb = seq // ATTN_BLOCK
    kv_w = 2 * KV_WIDTH
    assert c_len % ATTN_BLOCK == 0 and WINDOW == ATTN_BLOCK
    n_keys = 3 * ATTN_BLOCK + c_len
    rows = HEADS_PER_KV * ATTN_BLOCK
    bias = _band_bias()
    bias_block = (None, None, ATTN_BLOCK, ATTN_BLOCK)
    return pl.pallas_call(
        _attn_kernel,
        grid=(bsz, nb),
        in_specs=[pl.BlockSpec(memory_space=pltpu.SMEM),
                  pl.BlockSpec((ATTN_BLOCK, ATTN_WIDTH), lambda b, k: (b * nb + k, 0)),
                  pl.BlockSpec((ATTN_BLOCK, kv_w), lambda b, k: (b * nb + jnp.maximum(k - 1, 0), 0)),
                  pl.BlockSpec((ATTN_BLOCK, kv_w), lambda b, k: (b * nb + k, 0)),
                  pl.BlockSpec((ATTN_BLOCK, kv_w), lambda b, k: (b * nb + jnp.minimum(k + 1, nb - 1), 0)),
                  pl.BlockSpec((c_len, kv_w), lambda b, k: (b, 0)),
                  pl.BlockSpec(bias_block, lambda b, k: (0, (k == 0).astype(jnp.int32), 0, 0)),
                  pl.BlockSpec(bias_block, lambda b, k: (1, (k == nb - 1).astype(jnp.int32), 0, 0))],
        out_specs=pl.BlockSpec((ATTN_BLOCK, ATTN_WIDTH), lambda b, k: (b * nb + k, 0)),
        out_shape=jax.ShapeDtypeStruct((n_tok, ATTN_WIDTH), BF16),
        scratch_shapes=[pltpu.VMEM((N_KV_HEADS, rows, n_keys), F32),
                        pltpu.VMEM((N_KV_HEADS, rows, n_keys), BF16)],
        compiler_params=_params("parallel", "arbitrary"),
        name="attn",
    )(sink[lyr], q, kv, kv, kv, kv_ctx, bias, bias)


def _merge_kernel(x_ref, mod_ref, g_ref, a_ref, b_ref, ga_ref, gb_ref, wa_ref, wb_ref, wo_ref, o_ref):
    merged = (ga_ref[...].astype(F32) * _dot(a_ref[...], wa_ref[...])
              + gb_ref[...].astype(F32) * _dot(b_ref[...], wb_ref[...]))
    y = _dot(merged.astype(BF16), wo_ref[...])
    o_ref[...] = x_ref[...] + mod_ref[0, 5:6, :] * (_rms(y) * g_ref[3:4, :])


def _merge(x, mod3, norm_g, mix_a, mix_b, gates, w_a, w_b, w_o, lyr, seq, tm):
    n_tok, d = x.shape
    tiles_per_seq = seq // tm
    weight = lambda w: pl.BlockSpec((None,) + w.shape[1:], lambda i: (lyr, 0, 0), pipeline_mode=pl.Buffered(1))
    return pl.pallas_call(
        _merge_kernel,
        grid=(n_tok // tm,),
        in_specs=[pl.BlockSpec((tm, d), lambda i: (i, 0)),
                  pl.BlockSpec((1, N_MOD, d), lambda i: (i // tiles_per_seq, 0, 0)),
                  pl.BlockSpec(norm_g.shape, lambda i: (0, 0)),
                  pl.BlockSpec((tm, GMLP_WIDTH), lambda i: (i, 0)),
                  pl.BlockSpec((tm, ATTN_WIDTH), lambda i: (i, 0)),
                  pl.BlockSpec((tm, d), lambda i: (i, 0)),
                  pl.BlockSpec((tm, d), lambda i: (i, 1)),
                  weight(w_a), weight(w_b), weight(w_o)],
        out_specs=pl.BlockSpec((tm, d), lambda i: (i, 0)),
        out_shape=jax.ShapeDtypeStruct((n_tok, d), F32),
        compiler_params=_params("parallel"),
        name="merge",
    )(x, mod3, norm_g, mix_a, mix_b, gates, gates, w_a, w_b, w_o)


def _rope_tables(seq):
    rows = seq // GRID_W
    row = jnp.repeat(jnp.arange(rows), GRID_W).astype(F32)
    col = jnp.tile(jnp.arange(GRID_W), rows).astype(F32)
    quarter = HEAD_DIM // 4
    freqs = ROPE_BASE ** (-jnp.arange(quarter, dtype=F32) / quarter)
    ang_r = row[:, None] * freqs[None, :]
    ang_c = col[:, None] * freqs[None, :]
    ang = jnp.concatenate([ang_r, ang_r, ang_c, ang_c], axis=-1)
    cos, sin = jnp.cos(ang), jnp.sin(ang)
    first = (jnp.arange(HEAD_DIM) // quarter) % 2 == 0
    return cos, jnp.where(first, -sin, 0.0), jnp.where(first, 0.0, sin)


def kernel(x, c, ctx, c_ctx, w_ada, b_ada, norm_g, w_ffn_in, w_ffn_out, w_in, gmlp_ln_g, gmlp_ln_b,
           w_spatial, b_spatial, attn_sink, w_branch_a, w_branch_b, w_out):
    bsz, seq, d = x.shape
    c_len = ctx.shape[1]
    depth = w_ada.shape[0]
    ctx_row = bsz
    assert bsz < ADA_ROWS and seq % 512 == 0 and (bsz * c_len) % 256 == 0
    assert depth == 1, "context-stream outputs of a non-final layer are not implemented"

    cos, sin_lo, sin_hi = _rope_tables(seq)
    c_all = jnp.zeros((ADA_ROWS, d), F32).at[:bsz].set(c).at[ctx_row].set(c_ctx)
    xt = x.reshape(bsz * seq, d)
    ct = ctx.reshape(bsz * c_len, d)
    tm_ffn = 1024
    tf_ffn = 512
    tm_ctx = min(1024, bsz * c_len)
    assert seq % tm_ffn == 0 and (bsz * c_len) % tm_ctx == 0
    x_row = lambda i: i // (seq // tm_ffn)
    c_row = lambda i: ctx_row

    wf_in = w_ffn_in.astype(BF16)
    wf_out = w_ffn_out.astype(BF16)
    w_in_b = w_in.astype(BF16)
    w_s = w_spatial.astype(BF16)
    w_a, w_b, w_o = w_branch_a.astype(BF16), w_branch_b.astype(BF16), w_out.astype(BF16)

    for l in range(depth):
        g = norm_g[l]
        b_full = jnp.repeat(b_spatial[l].T, GMLP_GROUP_DIM, axis=1)
        mod3 = _ada(c_all, w_ada, b_ada, l, d // 2).reshape(ADA_ROWS, N_MOD, d)

        mod8 = jnp.broadcast_to(mod3[:, :, None, :], (ADA_ROWS, N_MOD, SUBLANES, d))
        g8 = jnp.broadcast_to(g[:, None, :], (g.shape[0], SUBLANES, d))

        xt = _ffn(xt, mod8, g8, wf_in, wf_out, l, 0, x_row, tm_ffn, tf_ffn)
        ct = _ffn(ct, mod8, g8, wf_in, wf_out, l, 0, c_row, tm_ctx, tf_ffn)

        q, kv, ug, gates = _proj(xt, mod3, g, cos, sin_lo, sin_hi, w_in_b, l, seq, 256)
        kv_ctx = _ctx_kv(ct, mod3, g, w_in_b, l, ctx_row, min(256, bsz * c_len))
        mix_a = _gmlp(ug, gmlp_ln_g, gmlp_ln_b, w_s, b_full, l, 512)
        mix_b = _attn(q, kv, kv_ctx, attn_sink, l, bsz, seq, c_len)
        xt = _merge(xt, mod3, g, mix_a, mix_b, gates, w_a, w_b, w_o, l, seq, 256)

        xt = _ffn(xt, mod8, g8, wf_in, wf_out, l, 2, x_row, tm_ffn, tf_ffn)
    return xt.reshape(bsz, seq, d)
```

```python
import functools

import jax
import jax.numpy as jnp
from jax import lax
from jax.experimental import pallas as pl
from jax.experimental.pallas import tpu as pltpu

GRID_W = 64
HEAD_DIM = 128
N_HEADS = 8
N_KV_HEADS = 2
HEADS_PER_KV = N_HEADS // N_KV_HEADS
ATTN_WIDTH = N_HEADS * HEAD_DIM
KV_WIDTH = N_KV_HEADS * HEAD_DIM
WINDOW = 128
ATTN_BLOCK = 128
ROPE_BASE = 10000.0
CHUNK = 128
N_GMLP_GROUPS = 8
GMLP_WIDTH = 1024
GMLP_GROUP_DIM = GMLP_WIDTH // N_GMLP_GROUPS
FFN_RES_WEIGHT = 0.5
N_MOD = 9
NORM_EPS = 1e-6
NEG_INF = -1e30

ADA_ROWS = 8
V7X_VMEM_LIMIT = 58 * 1024 * 1024
PROJ_COLS = 512
SUBLANES = 8
ROW_CHUNK = 16
ROW_CHUNK_UNROLL = 8
STATS_UNROLL = 16
FFN_ACT_COLS = 256
ATTN_ROW_CHUNK = 32

F32 = jnp.float32
BF16 = jnp.bfloat16


def _params(*sem):
    return pltpu.CompilerParams(dimension_semantics=sem, vmem_limit_bytes=V7X_VMEM_LIMIT)


def _dot(a, b):
    return jnp.dot(a, b, preferred_element_type=F32)


def _rms(t):
    return t * lax.rsqrt(jnp.mean(t * t, axis=-1, keepdims=True) + NORM_EPS)


def _chunk_rows(v8):
    return jnp.concatenate([v8] * (ROW_CHUNK // SUBLANES), axis=0)


def _for_row_chunks(n_rows, body, unroll=ROW_CHUNK_UNROLL):
    def step(k, carry):
        body(pl.ds(pl.multiple_of(k * ROW_CHUNK, ROW_CHUNK), ROW_CHUNK))
        return carry
    lax.fori_loop(0, n_rows // ROW_CHUNK, step, 0, unroll=unroll)


def _ada_kernel(c_ref, w_ref, b_ref, o_ref):
    c = c_ref[...]
    a = (c * jax.nn.sigmoid(c)).astype(BF16)
    o_ref[...] = _dot(a, w_ref[...].astype(BF16)) + b_ref[...]


def _ada(c_all, w_ada, b_ada, lyr, tn):
    _, d, n = w_ada.shape
    assert n % tn == 0
    return pl.pallas_call(
        _ada_kernel,
        grid=(n // tn,),
        in_specs=[pl.BlockSpec((ADA_ROWS, d), lambda j: (0, 0)),
                  pl.BlockSpec((None, d, tn), lambda j: (lyr, 0, j)),
                  pl.BlockSpec((1, tn), lambda j: (lyr, j))],
        out_specs=pl.BlockSpec((ADA_ROWS, tn), lambda j: (0, j)),
        out_shape=jax.ShapeDtypeStruct((ADA_ROWS, n), F32),
        compiler_params=_params("arbitrary"),
        name="ada",
    )(c_all, w_ada, b_ada)


def _ffn_kernel(sub, t_ref, mod_ref, g_ref, wg_ref, wu_ref, wo_ref, o_ref, h_ref, rstd_ref):
    j = pl.program_id(1)
    tm = t_ref.shape[0]

    @pl.when(j == 0)
    def _():
        def body(rows):
            y = _rms(t_ref[rows, :]) * _chunk_rows(g_ref[2 * sub])
            h = y * _chunk_rows(1.0 + mod_ref[0, 3 * sub + 1]) + _chunk_rows(mod_ref[0, 3 * sub])
            h_ref[rows, :] = h.astype(BF16)
            o_ref[rows, :] = jnp.zeros((ROW_CHUNK, o_ref.shape[1]), F32)
        _for_row_chunks(tm, body)

    h = h_ref[...]
    tf = wg_ref.shape[1]
    acts = []
    for c0 in range(0, tf, FFN_ACT_COLS):
        gt = _dot(h, wg_ref[:, c0:c0 + FFN_ACT_COLS])
        up = _dot(h, wu_ref[:, c0:c0 + FFN_ACT_COLS])
        acts.append(((gt * jax.nn.sigmoid(gt)) * up).astype(BF16))
    o_ref[...] += _dot(jnp.concatenate(acts, axis=1), wo_ref[...])

    @pl.when(j == pl.num_programs(1) - 1)
    def _():
        def stats(rows):
            a = o_ref[rows, :]
            rstd_ref[rows, :] = lax.rsqrt(jnp.mean(a * a, axis=-1, keepdims=True) + NORM_EPS)
        _for_row_chunks(tm, stats, unroll=STATS_UNROLL)

        def finish(rows):
            y = (o_ref[rows, :] * rstd_ref[rows, :]) * _chunk_rows(g_ref[2 * sub + 1])
            o_ref[rows, :] = t_ref[rows, :] + _chunk_rows(FFN_RES_WEIGHT * mod_ref[0, 3 * sub + 2]) * y
        _for_row_chunks(tm, finish)


def _ffn(t, mod8, g8, w_in, w_out, sub, mod_row, tm, tf):
    n_tok, d = t.shape
    f = w_out.shape[0]
    nf = f // tf
    return pl.pallas_call(
        functools.partial(_ffn_kernel, sub),
        grid=(n_tok // tm, nf),
        in_specs=[pl.BlockSpec((tm, d), lambda i, j: (i, 0)),
                  pl.BlockSpec((1, N_MOD, SUBLANES, d), lambda i, j: (mod_row(i), 0, 0, 0)),
                  pl.BlockSpec(g8.shape, lambda i, j: (0, 0, 0)),
                  pl.BlockSpec((d, tf), lambda i, j: (0, j)),
                  pl.BlockSpec((d, tf), lambda i, j: (0, nf + j)),
                  pl.BlockSpec((tf, d), lambda i, j: (j, 0))],
        out_specs=pl.BlockSpec((tm, d), lambda i, j: (i, 0)),
        out_shape=jax.ShapeDtypeStruct((n_tok, d), F32),
        scratch_shapes=[pltpu.VMEM((tm, d), BF16), pltpu.VMEM((tm, 1), F32)],
        compiler_params=_params("parallel", "arbitrary"),
        name=f"ffn{sub}",
    )(t, mod8, g8, w_in, w_in, w_out)


def _rope(x, cos, sin_lo, sin_hi):
    return (x * cos + pltpu.roll(x, HEAD_DIM - HEAD_DIM // 4, 1) * sin_lo
            + pltpu.roll(x, HEAD_DIM // 4, 1) * sin_hi)


def _modulated(t_ref, mod_ref, g_ref):
    y = _rms(t_ref[...]) * g_ref[2:3, :]
    return y * (1.0 + mod_ref[0, 4:5, :]) + mod_ref[0, 3:4, :]


def _proj_kernel(t_ref, mod_ref, g_ref, cos_ref, slo_ref, shi_ref, w_ref,
                 q_ref, kv_ref, ug_ref, gate_ref, h_ref):
    h_ref[...] = _modulated(t_ref, mod_ref, g_ref).astype(BF16)
    tn = PROJ_COLS

    def cols(c):
        return _dot(h_ref[...], w_ref[:, c * tn:(c + 1) * tn])

    def rope_into(o_ref, p, col0, n_heads):
        cos, slo, shi = cos_ref[...], slo_ref[...], shi_ref[...]
        for a in range(n_heads):
            x = p[:, a * HEAD_DIM:(a + 1) * HEAD_DIM]
            o_ref[:, col0 + a * HEAD_DIM:col0 + (a + 1) * HEAD_DIM] = _rope(x, cos, slo, shi).astype(BF16)

    c = 0
    for k in range(ATTN_WIDTH // tn):
        rope_into(q_ref, cols(c), k * tn, tn // HEAD_DIM)
        c += 1
    p = cols(c)
    rope_into(kv_ref, p, 0, N_KV_HEADS)
    kv_ref[:, KV_WIDTH:] = p[:, KV_WIDTH:].astype(BF16)
    c += 1
    for k in range(ug_ref.shape[1] // tn):
        ug_ref[:, k * tn:(k + 1) * tn] = jax.nn.gelu(cols(c)).astype(BF16)
        c += 1
    for k in range(gate_ref.shape[1] // tn):
        gate_ref[:, k * tn:(k + 1) * tn] = jax.nn.sigmoid(cols(c)).astype(BF16)
        c += 1


def _proj(x, mod3, norm_g, cos, sin_lo, sin_hi, w_in, lyr, seq, tm):
    n_tok, d = x.shape
    p_width = w_in.shape[2]
    assert PROJ_COLS == 2 * KV_WIDTH and ATTN_WIDTH % PROJ_COLS == 0 and d % PROJ_COLS == 0
    assert p_width == ATTN_WIDTH + 2 * KV_WIDTH + 2 * GMLP_WIDTH + 2 * d
    tiles_per_seq = seq // tm
    hd_spec = pl.BlockSpec((tm, HEAD_DIM), lambda i: (i % tiles_per_seq, 0))
    widths = (ATTN_WIDTH, 2 * KV_WIDTH, 2 * GMLP_WIDTH, 2 * d)
    return pl.pallas_call(
        _proj_kernel,
        grid=(n_tok // tm,),
        in_specs=[pl.BlockSpec((tm, d), lambda i: (i, 0)),
                  pl.BlockSpec((1, N_MOD, d), lambda i: (i // tiles_per_seq, 0, 0)),
                  pl.BlockSpec(norm_g.shape, lambda i: (0, 0)),
                  hd_spec, hd_spec, hd_spec,
                  pl.BlockSpec((None, d, p_width), lambda i: (lyr, 0, 0), pipeline_mode=pl.Buffered(1))],
        out_specs=[pl.BlockSpec((tm, w), lambda i: (i, 0)) for w in widths],
        out_shape=[jax.ShapeDtypeStruct((n_tok, w), BF16) for w in widths],
        scratch_shapes=[pltpu.VMEM((tm, d), BF16)],
        compiler_params=_params("parallel"),
        name="proj",
    )(x, mod3, norm_g, cos, sin_lo, sin_hi, w_in)


def _ctx_kv_kernel(t_ref, mod_ref, g_ref, w_ref, kv_ref):
    kv_ref[...] = _dot(_modulated(t_ref, mod_ref, g_ref).astype(BF16), w_ref[...]).astype(BF16)


def _ctx_kv(ctx, mod3, norm_g, w_in, lyr, ctx_row, tm):
    n_tok, d = ctx.shape
    tn = 2 * KV_WIDTH
    return pl.pallas_call(
        _ctx_kv_kernel,
        grid=(n_tok // tm,),
        in_specs=[pl.BlockSpec((tm, d), lambda i: (i, 0)),
                  pl.BlockSpec((1, N_MOD, d), lambda i: (ctx_row, 0, 0)),
                  pl.BlockSpec(norm_g.shape, lambda i: (0, 0)),
                  pl.BlockSpec((None, d, tn), lambda i: (lyr, 0, ATTN_WIDTH // tn))],
        out_specs=pl.BlockSpec((tm, tn), lambda i: (i, 0)),
        out_shape=jax.ShapeDtypeStruct((n_tok, tn), BF16),
        compiler_params=_params("parallel"),
        name="ctx_kv",
    )(ctx, mod3, norm_g, w_in)


def _gmlp_rows(u_ref, v_ref, lg_ref, lb_ref, ws_ref, bs_ref, o_ref):
    tm = u_ref.shape[0]
    for c in range(tm // CHUNK):
        rows = slice(c * CHUNK, (c + 1) * CHUNK)
        v = v_ref[rows, :].astype(F32)
        mu = jnp.mean(v, axis=-1, keepdims=True)
        vc = v - mu
        var = jnp.mean(vc * vc, axis=-1, keepdims=True)
        vn = ((vc * lax.rsqrt(var + NORM_EPS)) * lg_ref[...] + lb_ref[...]).astype(BF16)
        for g in range(N_GMLP_GROUPS):
            cols = slice(g * GMLP_GROUP_DIM, (g + 1) * GMLP_GROUP_DIM)
            mixed = _dot(ws_ref[g], vn[:, cols]) + bs_ref[:, cols]
            o_ref[rows, cols] = (u_ref[rows, cols].astype(F32) * mixed).astype(BF16)


def _attn_kernel(n_cast, sink_ref, q_ref, kvp_ref, kvc_ref, kvn_ref, kvx_ref, bprev_ref, bnext_ref, *rest):
    cast_src, (o_ref, *cast_dst), (s_ref, p_ref) = rest[:n_cast], rest[n_cast:2 * n_cast + 1], rest[2 * n_cast + 1:]
    for src, dst in zip(cast_src, cast_dst):
        dst[...] = src[...].astype(BF16)
    blk_q = q_ref.shape[0]
    rows = HEADS_PER_KV * blk_q
    rc = ATTN_ROW_CHUNK
    kv_refs = (kvp_ref, kvc_ref, kvn_ref, kvx_ref)
    log2e = 1.4426950408889634
    scale2 = HEAD_DIM ** -0.5 * log2e
    nt = (((1,), (1,)), ((), ()))
    n_pieces = s_ref.shape[2] // ATTN_BLOCK
    group_heads = [range(g * HEADS_PER_KV, (g + 1) * HEADS_PER_KV) for g in range(N_KV_HEADS)]

    for g, heads in enumerate(group_heads):
        kc = slice(g * HEAD_DIM, (g + 1) * HEAD_DIM)
        q = jnp.concatenate([q_ref[:, a * HEAD_DIM:(a + 1) * HEAD_DIM] for a in heads], axis=0)
        k_all = jnp.concatenate([r[:, kc] for r in kv_refs], axis=0)
        s_ref[g] = lax.dot_general(q, k_all, nt, preferred_element_type=F32)

    for g, heads in enumerate(group_heads):
        invs = []
        for r in range(0, rows, rc):
            sink2 = sink_ref[heads[r // blk_q]] * log2e
            rq = slice(r % blk_q, r % blk_q + rc)
            pieces = [s_ref[g, r:r + rc, n * ATTN_BLOCK:(n + 1) * ATTN_BLOCK] for n in range(n_pieces)]
            pieces[0] = pieces[0] + bprev_ref[rq, :]
            pieces[2] = pieces[2] + bnext_ref[rq, :]
            m = jnp.max(functools.reduce(jnp.maximum, pieces), axis=-1, keepdims=True)
            m2 = jnp.maximum(m * scale2, sink2)
            es = [jnp.exp2(p * scale2 - m2) for p in pieces]
            denom = jnp.sum(functools.reduce(jnp.add, es), axis=-1, keepdims=True) + jnp.exp2(sink2 - m2)
            invs.append(1.0 / denom)
            for n, e in enumerate(es):
                p_ref[g, r:r + rc, n * ATTN_BLOCK:(n + 1) * ATTN_BLOCK] = e.astype(BF16)
        vc = slice(KV_WIDTH + g * HEAD_DIM, KV_WIDTH + (g + 1) * HEAD_DIM)
        v_all = jnp.concatenate([r[:, vc] for r in kv_refs], axis=0)
        out = _dot(p_ref[g], v_all) * jnp.concatenate(invs, axis=0)
        for r, a in enumerate(heads):
            o_ref[:, a * HEAD_DIM:(a + 1) * HEAD_DIM] = out[r * blk_q:(r + 1) * blk_q, :].astype(BF16)


def _band_bias():
    qi = jnp.arange(ATTN_BLOCK)[:, None]
    kj = jnp.arange(ATTN_BLOCK)[None, :]
    neg = jnp.full((ATTN_BLOCK, ATTN_BLOCK), NEG_INF, F32)
    prev = jnp.where(kj >= qi, 0.0, NEG_INF).astype(F32)
    nxt = jnp.where(kj <= qi, 0.0, NEG_INF).astype(F32)
    return jnp.stack([jnp.stack([prev, neg]), jnp.stack([nxt, neg])])


def _attn(q, kv, kv_ctx, sink, lyr, bsz, seq, c_len, cast_srcs):
    n_tok = q.shape[0]
    nb = seq // ATTN_BLOCK
    kv_w = 2 * KV_WIDTH
    assert c_len % ATTN_BLOCK == 0 and WINDOW == ATTN_BLOCK
    n_keys = 3 * ATTN_BLOCK + c_len
    rows = HEADS_PER_KV * ATTN_BLOCK
    bias = _band_bias()
    bias_block = (None, None, ATTN_BLOCK, ATTN_BLOCK)
    n_steps = bsz * nb
    cast_shapes = [a.shape[2:] for a, _ in cast_srcs]
    assert all(r % (n_steps * ROW_CHUNK) == 0 for r, _ in cast_shapes)
    cast_out_specs = [pl.BlockSpec((r // n_steps, w), lambda b, k: (b * nb + k, 0)) for r, w in cast_shapes]
    cast_in_specs = [pl.BlockSpec((None, None, r // n_steps, w), lambda b, k, lead=lead: (*lead, b * nb + k, 0))
                     for (r, w), (_, lead) in zip(cast_shapes, cast_srcs)]
    return pl.pallas_call(
        functools.partial(_attn_kernel, len(cast_srcs)),
        grid=(bsz, nb),
        in_specs=[pl.BlockSpec(memory_space=pltpu.SMEM),
                  pl.BlockSpec((ATTN_BLOCK, ATTN_WIDTH), lambda b, k: (b * nb + k, 0)),
                  pl.BlockSpec((ATTN_BLOCK, kv_w), lambda b, k: (b * nb + jnp.maximum(k - 1, 0), 0)),
                  pl.BlockSpec((ATTN_BLOCK, kv_w), lambda b, k: (b * nb + k, 0)),
                  pl.BlockSpec((ATTN_BLOCK, kv_w), lambda b, k: (b * nb + jnp.minimum(k + 1, nb - 1), 0)),
                  pl.BlockSpec((c_len, kv_w), lambda b, k: (b, 0)),
                  pl.BlockSpec(bias_block, lambda b, k: (0, (k == 0).astype(jnp.int32), 0, 0)),
                  pl.BlockSpec(bias_block, lambda b, k: (1, (k == nb - 1).astype(jnp.int32), 0, 0))]
                 + cast_in_specs,
        out_specs=[pl.BlockSpec((ATTN_BLOCK, ATTN_WIDTH), lambda b, k: (b * nb + k, 0))] + cast_out_specs,
        out_shape=[jax.ShapeDtypeStruct((n_tok, ATTN_WIDTH), BF16)]
                  + [jax.ShapeDtypeStruct(s, BF16) for s in cast_shapes],
        scratch_shapes=[pltpu.VMEM((N_KV_HEADS, rows, n_keys), F32),
                        pltpu.VMEM((N_KV_HEADS, rows, n_keys), BF16)],
        compiler_params=_params("parallel", "arbitrary"),
        name="attn",
    )(sink[lyr], q, kv, kv, kv, kv_ctx, bias, bias, *[a for a, _ in cast_srcs])


def _merge_kernel(x_ref, mod_ref, g_ref, u_ref, v_ref, lg_ref, lb_ref, ws_ref, bs_ref, b_ref, ga_ref, gb_ref,
                  wa_ref, wb_ref, wo_ref, o_ref, a_ref):
    branch_b = gb_ref[...].astype(F32) * _dot(b_ref[...], wb_ref[...])
    _gmlp_rows(u_ref, v_ref, lg_ref, lb_ref, ws_ref, bs_ref, a_ref)
    merged = ga_ref[...].astype(F32) * _dot(a_ref[...], wa_ref[...]) + branch_b
    y = _dot(merged.astype(BF16), wo_ref[...])
    o_ref[...] = x_ref[...] + mod_ref[0, 5:6, :] * (_rms(y) * g_ref[3:4, :])


def _merge(x, mod3, norm_g, ug, ln_g, ln_b, w_s, b_full, mix_b, gates, w_a, w_b, w_o, lyr, seq, tm):
    n_tok, d = x.shape
    assert tm % CHUNK == 0
    tiles_per_seq = seq // tm
    weight = lambda w: pl.BlockSpec((None,) + w.shape[1:], lambda i: (lyr,) + (0,) * (w.ndim - 1),
                                    pipeline_mode=pl.Buffered(1))
    return pl.pallas_call(
        _merge_kernel,
        grid=(n_tok // tm,),
        in_specs=[pl.BlockSpec((tm, d), lambda i: (i, 0)),
                  pl.BlockSpec((1, N_MOD, d), lambda i: (i // tiles_per_seq, 0, 0)),
                  pl.BlockSpec(norm_g.shape, lambda i: (0, 0)),
                  pl.BlockSpec((tm, GMLP_WIDTH), lambda i: (i, 0)),
                  pl.BlockSpec((tm, GMLP_WIDTH), lambda i: (i, 1)),
                  pl.BlockSpec((1, GMLP_WIDTH), lambda i: (lyr, 0)),
                  pl.BlockSpec((1, GMLP_WIDTH), lambda i: (lyr, 0)),
                  weight(w_s),
                  pl.BlockSpec(b_full.shape, lambda i: (0, 0)),
                  pl.BlockSpec((tm, ATTN_WIDTH), lambda i: (i, 0)),
                  pl.BlockSpec((tm, d), lambda i: (i, 0)),
                  pl.BlockSpec((tm, d), lambda i: (i, 1)),
                  weight(w_a), weight(w_b), weight(w_o)],
        out_specs=pl.BlockSpec((tm, d), lambda i: (i, 0)),
        out_shape=jax.ShapeDtypeStruct((n_tok, d), F32),
        scratch_shapes=[pltpu.VMEM((tm, GMLP_WIDTH), BF16)],
        compiler_params=_params("parallel"),
        name="merge",
    )(x, mod3, norm_g, ug, ug, ln_g, ln_b, w_s, b_full, mix_b, gates, gates, w_a, w_b, w_o)


def _rope_tables(seq):
    rows = seq // GRID_W
    row = jnp.repeat(jnp.arange(rows), GRID_W).astype(F32)
    col = jnp.tile(jnp.arange(GRID_W), rows).astype(F32)
    quarter = HEAD_DIM // 4
    freqs = ROPE_BASE ** (-jnp.arange(quarter, dtype=F32) / quarter)
    ang_r = row[:, None] * freqs[None, :]
    ang_c = col[:, None] * freqs[None, :]
    ang = jnp.concatenate([ang_r, ang_r, ang_c, ang_c], axis=-1)
    cos, sin = jnp.cos(ang), jnp.sin(ang)
    first = (jnp.arange(HEAD_DIM) // quarter) % 2 == 0
    return cos, jnp.where(first, -sin, 0.0), jnp.where(first, 0.0, sin)


def kernel(x, c, ctx, c_ctx, w_ada, b_ada, norm_g, w_ffn_in, w_ffn_out, w_in, gmlp_ln_g, gmlp_ln_b,
           w_spatial, b_spatial, attn_sink, w_branch_a, w_branch_b, w_out):
    bsz, seq, d = x.shape
    c_len = ctx.shape[1]
    depth = w_ada.shape[0]
    ctx_row = bsz
    assert bsz < ADA_ROWS and seq % 512 == 0 and (bsz * c_len) % 256 == 0
    assert depth == 1, "context-stream outputs of a non-final layer are not implemented"

    cos, sin_lo, sin_hi = _rope_tables(seq)
    c_all = jnp.zeros((ADA_ROWS, d), F32).at[:bsz].set(c).at[ctx_row].set(c_ctx)
    xt = x.reshape(bsz * seq, d)
    ct = ctx.reshape(bsz * c_len, d)
    tm_ffn = 1024
    tf_ffn = 512
    tm_ctx = min(1024, bsz * c_len)
    assert seq % tm_ffn == 0 and (bsz * c_len) % tm_ctx == 0
    x_row = lambda i: i // (seq // tm_ffn)
    c_row = lambda i: ctx_row

    w_in_b = w_in.astype(BF16)
    w_s = w_spatial.astype(BF16)
    w_a, w_b, w_o = w_branch_a.astype(BF16), w_branch_b.astype(BF16), w_out.astype(BF16)

    for l in range(depth):
        g = norm_g[l]
        b_full = jnp.repeat(b_spatial[l].T, GMLP_GROUP_DIM, axis=1)
        mod3 = _ada(c_all, w_ada, b_ada, l, d // 2).reshape(ADA_ROWS, N_MOD, d)

        mod8 = jnp.broadcast_to(mod3[:, :, None, :], (ADA_ROWS, N_MOD, SUBLANES, d))
        g8 = jnp.broadcast_to(g[:, None, :], (g.shape[0], SUBLANES, d))

        wf_in0, wf_out0 = w_ffn_in[l, 0].astype(BF16), w_ffn_out[l, 0].astype(BF16)
        xt = _ffn(xt, mod8, g8, wf_in0, wf_out0, 0, x_row, tm_ffn, tf_ffn)
        ct = _ffn(ct, mod8, g8, wf_in0, wf_out0, 0, c_row, tm_ctx, tf_ffn)

        q, kv, ug, gates = _proj(xt, mod3, g, cos, sin_lo, sin_hi, w_in_b, l, seq, 256)
        kv_ctx = _ctx_kv(ct, mod3, g, w_in_b, l, ctx_row, min(256, bsz * c_len))
        f_out = w_ffn_out.shape[2]
        mix_b, wf_in1, wf_out1 = _attn(q, kv, kv_ctx, attn_sink, l, bsz, seq, c_len,
                                       [(w_ffn_in, (l, 1)), (w_ffn_out.reshape(depth, 2, d, f_out), (l, 1))])
        xt = _merge(xt, mod3, g, ug, gmlp_ln_g, gmlp_ln_b, w_s, b_full, mix_b, gates, w_a, w_b, w_o, l, seq, 256)

        xt = _ffn(xt, mod8, g8, wf_in1, wf_out1.reshape(f_out, d), 2, x_row, tm_ffn, tf_ffn)
    return xt.reshape(bsz, seq, d)
```

```python
import functools

import jax
import jax.numpy as jnp
from jax import lax
from jax.experimental import pallas as pl
from jax.experimental.pallas import tpu as pltpu

GRID_W = 64
HEAD_DIM = 128
N_HEADS = 8
N_KV_HEADS = 2
HEADS_PER_KV = N_HEADS // N_KV_HEADS
ATTN_WIDTH = N_HEADS * HEAD_DIM
KV_WIDTH = N_KV_HEADS * HEAD_DIM
WINDOW = 128
ATTN_BLOCK = 128
ROPE_BASE = 10000.0
CHUNK = 128
N_GMLP_GROUPS = 8
GMLP_WIDTH = 1024
GMLP_GROUP_DIM = GMLP_WIDTH // N_GMLP_GROUPS
FFN_RES_WEIGHT = 0.5
N_MOD = 9
NORM_EPS = 1e-6
NEG_INF = -1e30

ADA_ROWS = 8
V7X_VMEM_LIMIT = 58 * 1024 * 1024
PROJ_COLS = 512
SUBLANES = 8
ROW_CHUNK = 16
ROW_CHUNK_UNROLL = 8
STATS_UNROLL = 16
FFN_ACT_COLS = 256
ATTN_ROW_CHUNK = 32

F32 = jnp.float32
BF16 = jnp.bfloat16


def _params(*sem):
    return pltpu.CompilerParams(dimension_semantics=sem, vmem_limit_bytes=V7X_VMEM_LIMIT)


def _dot(a, b):
    return jnp.dot(a, b, preferred_element_type=F32)


def _rms(t):
    return t * lax.rsqrt(jnp.mean(t * t, axis=-1, keepdims=True) + NORM_EPS)


def _chunk_rows(v8):
    return jnp.concatenate([v8] * (ROW_CHUNK // SUBLANES), axis=0)


def _for_row_chunks(n_rows, body, unroll=ROW_CHUNK_UNROLL):
    def step(k, carry):
        body(pl.ds(pl.multiple_of(k * ROW_CHUNK, ROW_CHUNK), ROW_CHUNK))
        return carry
    lax.fori_loop(0, n_rows // ROW_CHUNK, step, 0, unroll=unroll)


def _ada_kernel(c_ref, w_ref, b_ref, o_ref):
    c = c_ref[...]
    a = (c * jax.nn.sigmoid(c)).astype(BF16)
    o_ref[...] = _dot(a, w_ref[...].astype(BF16)) + b_ref[...]


def _ada(c_all, w_ada, b_ada, lyr, tn):
    _, d, n = w_ada.shape
    assert n % tn == 0
    return pl.pallas_call(
        _ada_kernel,
        grid=(n // tn,),
        in_specs=[pl.BlockSpec((ADA_ROWS, d), lambda j: (0, 0)),
                  pl.BlockSpec((None, d, tn), lambda j: (lyr, 0, j)),
                  pl.BlockSpec((1, tn), lambda j: (lyr, j))],
        out_specs=pl.BlockSpec((ADA_ROWS, tn), lambda j: (0, j)),
        out_shape=jax.ShapeDtypeStruct((ADA_ROWS, n), F32),
        compiler_params=_params("arbitrary"),
        name="ada",
    )(c_all, w_ada, b_ada)


def _ffn_kernel(sub, t_ref, mod_ref, g_ref, wg_ref, wu_ref, wo_ref, o_ref, h_ref, rstd_ref):
    j = pl.program_id(1)
    tm = t_ref.shape[0]

    @pl.when(j == 0)
    def _():
        def body(rows):
            y = _rms(t_ref[rows, :]) * _chunk_rows(g_ref[2 * sub])
            h = y * _chunk_rows(1.0 + mod_ref[0, 3 * sub + 1]) + _chunk_rows(mod_ref[0, 3 * sub])
            h_ref[rows, :] = h.astype(BF16)
            o_ref[rows, :] = jnp.zeros((ROW_CHUNK, o_ref.shape[1]), F32)
        _for_row_chunks(tm, body)

    h = h_ref[...]
    tf = wg_ref.shape[1]
    acts = []
    for c0 in range(0, tf, FFN_ACT_COLS):
        gt = _dot(h, wg_ref[:, c0:c0 + FFN_ACT_COLS])
        up = _dot(h, wu_ref[:, c0:c0 + FFN_ACT_COLS])
        acts.append(((gt * jax.nn.sigmoid(gt)) * up).astype(BF16))
    o_ref[...] += _dot(jnp.concatenate(acts, axis=1), wo_ref[...])

    @pl.when(j == pl.num_programs(1) - 1)
    def _():
        def stats(rows):
            a = o_ref[rows, :]
            rstd_ref[rows, :] = lax.rsqrt(jnp.mean(a * a, axis=-1, keepdims=True) + NORM_EPS)
        _for_row_chunks(tm, stats, unroll=STATS_UNROLL)

        def finish(rows):
            y = (o_ref[rows, :] * rstd_ref[rows, :]) * _chunk_rows(g_ref[2 * sub + 1])
            o_ref[rows, :] = t_ref[rows, :] + _chunk_rows(FFN_RES_WEIGHT * mod_ref[0, 3 * sub + 2]) * y
        _for_row_chunks(tm, finish)


def _ffn(t, mod8, g8, w_in, w_out, sub, mod_row, tm, tf):
    n_tok, d = t.shape
    f = w_out.shape[0]
    nf = f // tf
    return pl.pallas_call(
        functools.partial(_ffn_kernel, sub),
        grid=(n_tok // tm, nf),
        in_specs=[pl.BlockSpec((tm, d), lambda i, j: (i, 0)),
                  pl.BlockSpec((1, N_MOD, SUBLANES, d), lambda i, j: (mod_row(i), 0, 0, 0)),
                  pl.BlockSpec(g8.shape, lambda i, j: (0, 0, 0)),
                  pl.BlockSpec((d, tf), lambda i, j: (0, j)),
                  pl.BlockSpec((d, tf), lambda i, j: (0, nf + j)),
                  pl.BlockSpec((tf, d), lambda i, j: (j, 0))],
        out_specs=pl.BlockSpec((tm, d), lambda i, j: (i, 0)),
        out_shape=jax.ShapeDtypeStruct((n_tok, d), F32),
        scratch_shapes=[pltpu.VMEM((tm, d), BF16), pltpu.VMEM((tm, 1), F32)],
        compiler_params=_params("parallel", "arbitrary"),
        name=f"ffn{sub}",
    )(t, mod8, g8, w_in, w_in, w_out)


def _rope(x, cos, sin_lo, sin_hi):
    return (x * cos + pltpu.roll(x, HEAD_DIM - HEAD_DIM // 4, 1) * sin_lo
            + pltpu.roll(x, HEAD_DIM // 4, 1) * sin_hi)


def _modulated(t_ref, mod_ref, g_ref):
    y = _rms(t_ref[...]) * g_ref[2:3, :]
    return y * (1.0 + mod_ref[0, 4:5, :]) + mod_ref[0, 3:4, :]


def _proj_kernel(t_ref, mod_ref, g_ref, cos_ref, slo_ref, shi_ref, w_ref,
                 q_ref, kv_ref, ug_ref, gate_ref, h_ref):
    h_ref[...] = _modulated(t_ref, mod_ref, g_ref).astype(BF16)
    tn = PROJ_COLS

    def cols(c):
        return _dot(h_ref[...], w_ref[:, c * tn:(c + 1) * tn])

    def rope_into(o_ref, p, col0, n_heads):
        cos, slo, shi = cos_ref[...], slo_ref[...], shi_ref[...]
        for a in range(n_heads):
            x = p[:, a * HEAD_DIM:(a + 1) * HEAD_DIM]
            o_ref[:, col0 + a * HEAD_DIM:col0 + (a + 1) * HEAD_DIM] = _rope(x, cos, slo, shi).astype(BF16)

    c = 0
    for k in range(ATTN_WIDTH // tn):
        rope_into(q_ref, cols(c), k * tn, tn // HEAD_DIM)
        c += 1
    p = cols(c)
    rope_into(kv_ref, p, 0, N_KV_HEADS)
    kv_ref[:, KV_WIDTH:] = p[:, KV_WIDTH:].astype(BF16)
    c += 1
    for k in range(ug_ref.shape[1] // tn):
        ug_ref[:, k * tn:(k + 1) * tn] = jax.nn.gelu(cols(c)).astype(BF16)
        c += 1
    for k in range(gate_ref.shape[1] // tn):
        gate_ref[:, k * tn:(k + 1) * tn] = jax.nn.sigmoid(cols(c)).astype(BF16)
        c += 1


def _proj(x, mod3, norm_g, cos, sin_lo, sin_hi, w_in, lyr, seq, tm):
    n_tok, d = x.shape
    p_width = w_in.shape[2]
    assert PROJ_COLS == 2 * KV_WIDTH and ATTN_WIDTH % PROJ_COLS == 0 and d % PROJ_COLS == 0
    assert p_width == ATTN_WIDTH + 2 * KV_WIDTH + 2 * GMLP_WIDTH + 2 * d
    tiles_per_seq = seq // tm
    hd_spec = pl.BlockSpec((tm, HEAD_DIM), lambda i: (i % tiles_per_seq, 0))
    widths = (ATTN_WIDTH, 2 * KV_WIDTH, 2 * GMLP_WIDTH, 2 * d)
    return pl.pallas_call(
        _proj_kernel,
        grid=(n_tok // tm,),
        in_specs=[pl.BlockSpec((tm, d), lambda i: (i, 0)),
                  pl.BlockSpec((1, N_MOD, d), lambda i: (i // tiles_per_seq, 0, 0)),
                  pl.BlockSpec(norm_g.shape, lambda i: (0, 0)),
                  hd_spec, hd_spec, hd_spec,
                  pl.BlockSpec((None, d, p_width), lambda i: (lyr, 0, 0), pipeline_mode=pl.Buffered(1))],
        out_specs=[pl.BlockSpec((tm, w), lambda i: (i, 0)) for w in widths],
        out_shape=[jax.ShapeDtypeStruct((n_tok, w), BF16) for w in widths],
        scratch_shapes=[pltpu.VMEM((tm, d), BF16)],
        compiler_params=_params("parallel"),
        name="proj",
    )(x, mod3, norm_g, cos, sin_lo, sin_hi, w_in)


def _ctx_kv_kernel(t_ref, mod_ref, g_ref, w_ref, kv_ref):
    kv_ref[...] = _dot(_modulated(t_ref, mod_ref, g_ref).astype(BF16), w_ref[...]).astype(BF16)


def _ctx_kv(ctx, mod3, norm_g, w_in, lyr, ctx_row, tm):
    n_tok, d = ctx.shape
    tn = 2 * KV_WIDTH
    return pl.pallas_call(
        _ctx_kv_kernel,
        grid=(n_tok // tm,),
        in_specs=[pl.BlockSpec((tm, d), lambda i: (i, 0)),
                  pl.BlockSpec((1, N_MOD, d), lambda i: (ctx_row, 0, 0)),
                  pl.BlockSpec(norm_g.shape, lambda i: (0, 0)),
                  pl.BlockSpec((None, d, tn), lambda i: (lyr, 0, ATTN_WIDTH // tn))],
        out_specs=pl.BlockSpec((tm, tn), lambda i: (i, 0)),
        out_shape=jax.ShapeDtypeStruct((n_tok, tn), BF16),
        compiler_params=_params("parallel"),
        name="ctx_kv",
    )(ctx, mod3, norm_g, w_in)


def _gmlp_rows(u_ref, v_ref, lg_ref, lb_ref, ws_ref, bs_ref, o_ref):
    tm = u_ref.shape[0]
    for c in range(tm // CHUNK):
        rows = slice(c * CHUNK, (c + 1) * CHUNK)
        v = v_ref[rows, :].astype(F32)
        mu = jnp.mean(v, axis=-1, keepdims=True)
        vc = v - mu
        var = jnp.mean(vc * vc, axis=-1, keepdims=True)
        vn = ((vc * lax.rsqrt(var + NORM_EPS)) * lg_ref[...] + lb_ref[...]).astype(BF16)
        for g in range(N_GMLP_GROUPS):
            cols = slice(g * GMLP_GROUP_DIM, (g + 1) * GMLP_GROUP_DIM)
            mixed = _dot(ws_ref[g], vn[:, cols]) + bs_ref[:, cols]
            o_ref[rows, cols] = (u_ref[rows, cols].astype(F32) * mixed).astype(BF16)


def _attn_kernel(n_cast, sink_ref, q_ref, kvp_ref, kvc_ref, kvn_ref, kvx_ref, bprev_ref, bnext_ref, *rest):
    cast_src, (o_ref, *cast_dst), (s_ref, p_ref) = rest[:n_cast], rest[n_cast:2 * n_cast + 1], rest[2 * n_cast + 1:]
    for src, dst in zip(cast_src, cast_dst):
        dst[...] = src[...].astype(BF16)
    blk_q = q_ref.shape[0]
    rows = HEADS_PER_KV * blk_q
    rc = ATTN_ROW_CHUNK
    kv_refs = (kvp_ref, kvc_ref, kvn_ref, kvx_ref)
    log2e = 1.4426950408889634
    scale2 = HEAD_DIM ** -0.5 * log2e
    nt = (((1,), (1,)), ((), ()))
    n_pieces = s_ref.shape[2] // ATTN_BLOCK
    group_heads = [range(g * HEADS_PER_KV, (g + 1) * HEADS_PER_KV) for g in range(N_KV_HEADS)]

    for g, heads in enumerate(group_heads):
        kc = slice(g * HEAD_DIM, (g + 1) * HEAD_DIM)
        q = jnp.concatenate([q_ref[:, a * HEAD_DIM:(a + 1) * HEAD_DIM] for a in heads], axis=0)
        k_all = jnp.concatenate([r[:, kc] for r in kv_refs], axis=0)
        s_ref[g] = lax.dot_general(q, k_all, nt, preferred_element_type=F32)

    for g, heads in enumerate(group_heads):
        invs = []
        for r in range(0, rows, rc):
            sink2 = sink_ref[heads[r // blk_q]] * log2e
            rq = slice(r % blk_q, r % blk_q + rc)
            pieces = [s_ref[g, r:r + rc, n * ATTN_BLOCK:(n + 1) * ATTN_BLOCK] for n in range(n_pieces)]
            pieces[0] = pieces[0] + bprev_ref[rq, :]
            pieces[2] = pieces[2] + bnext_ref[rq, :]
            m = jnp.max(functools.reduce(jnp.maximum, pieces), axis=-1, keepdims=True)
            m2 = jnp.maximum(m * scale2, sink2)
            es = [jnp.exp2(p * scale2 - m2) for p in pieces]
            denom = jnp.sum(functools.reduce(jnp.add, es), axis=-1, keepdims=True) + jnp.exp2(sink2 - m2)
            invs.append(1.0 / denom)
            for n, e in enumerate(es):
                p_ref[g, r:r + rc, n * ATTN_BLOCK:(n + 1) * ATTN_BLOCK] = e.astype(BF16)
        vc = slice(KV_WIDTH + g * HEAD_DIM, KV_WIDTH + (g + 1) * HEAD_DIM)
        v_all = jnp.concatenate([r[:, vc] for r in kv_refs], axis=0)
        out = _dot(p_ref[g], v_all) * jnp.concatenate(invs, axis=0)
        for r, a in enumerate(heads):
            o_ref[:, a * HEAD_DIM:(a + 1) * HEAD_DIM] = out[r * blk_q:(r + 1) * blk_q, :].astype(BF16)


def _band_bias():
    qi = jnp.arange(ATTN_BLOCK)[:, None]
    kj = jnp.arange(ATTN_BLOCK)[None, :]
    neg = jnp.full((ATTN_BLOCK, ATTN_BLOCK), NEG_INF, F32)
    prev = jnp.where(kj >= qi, 0.0, NEG_INF).astype(F32)
    nxt = jnp.where(kj <= qi, 0.0, NEG_INF).astype(F32)
    return jnp.stack([jnp.stack([prev, neg]), jnp.stack([nxt, neg])])


def _attn(q, kv, kv_ctx, sink, lyr, bsz, seq, c_len, cast_srcs):
    n_tok = q.shape[0]
    nb = seq // ATTN_BLOCK
    kv_w = 2 * KV_WIDTH
    assert c_len % ATTN_BLOCK == 0 and WINDOW == ATTN_BLOCK
    n_keys = 3 * ATTN_BLOCK + c_len
    rows = HEADS_PER_KV * ATTN_BLOCK
    bias = _band_bias()
    bias_block = (None, None, ATTN_BLOCK, ATTN_BLOCK)
    n_steps = bsz * nb
    cast_shapes = [a.shape[2:] for a, _ in cast_srcs]
    cast_in_specs, cast_out_specs = [], []
    for (r, w), (_, lead) in zip(cast_shapes, cast_srcs):
        rb = next(c for c in range(ROW_CHUNK, r + 1, ROW_CHUNK) if r % c == 0 and r // c <= n_steps)
        slab = lambda b, k, last=r // rb - 1: jnp.minimum(b * nb + k, last)
        cast_in_specs.append(pl.BlockSpec((None, None, rb, w), lambda b, k, lead=lead, slab=slab: (*lead, slab(b, k), 0)))
        cast_out_specs.append(pl.BlockSpec((rb, w), lambda b, k, slab=slab: (slab(b, k), 0)))
    return pl.pallas_call(
        functools.partial(_attn_kernel, len(cast_srcs)),
        grid=(bsz, nb),
        in_specs=[pl.BlockSpec(memory_space=pltpu.SMEM),
                  pl.BlockSpec((ATTN_BLOCK, ATTN_WIDTH), lambda b, k: (b * nb + k, 0)),
                  pl.BlockSpec((ATTN_BLOCK, kv_w), lambda b, k: (b * nb + jnp.maximum(k - 1, 0), 0)),
                  pl.BlockSpec((ATTN_BLOCK, kv_w), lambda b, k: (b * nb + k, 0)),
                  pl.BlockSpec((ATTN_BLOCK, kv_w), lambda b, k: (b * nb + jnp.minimum(k + 1, nb - 1), 0)),
                  pl.BlockSpec((c_len, kv_w), lambda b, k: (b, 0)),
                  pl.BlockSpec(bias_block, lambda b, k: (0, (k == 0).astype(jnp.int32), 0, 0)),
                  pl.BlockSpec(bias_block, lambda b, k: (1, (k == nb - 1).astype(jnp.int32), 0, 0))]
                 + cast_in_specs,
        out_specs=[pl.BlockSpec((ATTN_BLOCK, ATTN_WIDTH), lambda b, k: (b * nb + k, 0))] + cast_out_specs,
        out_shape=[jax.ShapeDtypeStruct((n_tok, ATTN_WIDTH), BF16)]
                  + [jax.ShapeDtypeStruct(s, BF16) for s in cast_shapes],
        scratch_shapes=[pltpu.VMEM((N_KV_HEADS, rows, n_keys), F32),
                        pltpu.VMEM((N_KV_HEADS, rows, n_keys), BF16)],
        compiler_params=_params("arbitrary", "arbitrary"),
        name="attn",
    )(sink[lyr], q, kv, kv, kv, kv_ctx, bias, bias, *[a for a, _ in cast_srcs])


def _merge_kernel(x_ref, mod_ref, g_ref, u_ref, v_ref, lg_ref, lb_ref, ws_ref, bs_ref, b_ref, ga_ref, gb_ref,
                  wa_ref, wb_ref, wo_ref, o_ref, a_ref):
    branch_b = gb_ref[...].astype(F32) * _dot(b_ref[...], wb_ref[...])
    _gmlp_rows(u_ref, v_ref, lg_ref, lb_ref, ws_ref, bs_ref, a_ref)
    merged = ga_ref[...].astype(F32) * _dot(a_ref[...], wa_ref[...]) + branch_b
    y = _dot(merged.astype(BF16), wo_ref[...])
    o_ref[...] = x_ref[...] + mod_ref[0, 5:6, :] * (_rms(y) * g_ref[3:4, :])


def _merge(x, mod3, norm_g, ug, ln_g, ln_b, w_s, b_full, mix_b, gates, w_a, w_b, w_o, lyr, seq, tm):
    n_tok, d = x.shape
    assert tm % CHUNK == 0
    tiles_per_seq = seq // tm
    weight = lambda w: pl.BlockSpec((None,) + w.shape[1:], lambda i: (lyr,) + (0,) * (w.ndim - 1),
                                    pipeline_mode=pl.Buffered(1))
    return pl.pallas_call(
        _merge_kernel,
        grid=(n_tok // tm,),
        in_specs=[pl.BlockSpec((tm, d), lambda i: (i, 0)),
                  pl.BlockSpec((1, N_MOD, d), lambda i: (i // tiles_per_seq, 0, 0)),
                  pl.BlockSpec(norm_g.shape, lambda i: (0, 0)),
                  pl.BlockSpec((tm, GMLP_WIDTH), lambda i: (i, 0)),
                  pl.BlockSpec((tm, GMLP_WIDTH), lambda i: (i, 1)),
                  pl.BlockSpec((1, GMLP_WIDTH), lambda i: (lyr, 0)),
                  pl.BlockSpec((1, GMLP_WIDTH), lambda i: (lyr, 0)),
                  weight(w_s),
                  pl.BlockSpec(b_full.shape, lambda i: (0, 0)),
                  pl.BlockSpec((tm, ATTN_WIDTH), lambda i: (i, 0)),
                  pl.BlockSpec((tm, d), lambda i: (i, 0)),
                  pl.BlockSpec((tm, d), lambda i: (i, 1)),
                  weight(w_a), weight(w_b), weight(w_o)],
        out_specs=pl.BlockSpec((tm, d), lambda i: (i, 0)),
        out_shape=jax.ShapeDtypeStruct((n_tok, d), F32),
        scratch_shapes=[pltpu.VMEM((tm, GMLP_WIDTH), BF16)],
        compiler_params=_params("parallel"),
        name="merge",
    )(x, mod3, norm_g, ug, ug, ln_g, ln_b, w_s, b_full, mix_b, gates, gates, w_a, w_b, w_o)


def _rope_tables(seq):
    rows = seq // GRID_W
    row = jnp.repeat(jnp.arange(rows), GRID_W).astype(F32)
    col = jnp.tile(jnp.arange(GRID_W), rows).astype(F32)
    quarter = HEAD_DIM // 4
    freqs = ROPE_BASE ** (-jnp.arange(quarter, dtype=F32) / quarter)
    ang_r = row[:, None] * freqs[None, :]
    ang_c = col[:, None] * freqs[None, :]
    ang = jnp.concatenate([ang_r, ang_r, ang_c, ang_c], axis=-1)
    cos, sin = jnp.cos(ang), jnp.sin(ang)
    first = (jnp.arange(HEAD_DIM) // quarter) % 2 == 0
    return cos, jnp.where(first, -sin, 0.0), jnp.where(first, 0.0, sin)


def kernel(x, c, ctx, c_ctx, w_ada, b_ada, norm_g, w_ffn_in, w_ffn_out, w_in, gmlp_ln_g, gmlp_ln_b,
           w_spatial, b_spatial, attn_sink, w_branch_a, w_branch_b, w_out):
    bsz, seq, d = x.shape
    c_len = ctx.shape[1]
    depth = w_ada.shape[0]
    ctx_row = bsz
    assert bsz < ADA_ROWS and seq % 512 == 0 and (bsz * c_len) % 256 == 0
    assert depth == 1, "context-stream outputs of a non-final layer are not implemented"

    cos, sin_lo, sin_hi = _rope_tables(seq)
    c_all = jnp.zeros((ADA_ROWS, d), F32).at[:bsz].set(c).at[ctx_row].set(c_ctx)
    xt = x.reshape(bsz * seq, d)
    ct = ctx.reshape(bsz * c_len, d)
    tm_ffn = 1024
    tf_ffn = 512
    tm_ctx = min(1024, bsz * c_len)
    assert seq % tm_ffn == 0 and (bsz * c_len) % tm_ctx == 0
    x_row = lambda i: i // (seq // tm_ffn)
    c_row = lambda i: ctx_row

    w_in_b = w_in.astype(BF16)
    w_s = w_spatial.astype(BF16)
    w_a, w_b, w_o = w_branch_a.astype(BF16), w_branch_b.astype(BF16), w_out.astype(BF16)

    for l in range(depth):
        g = norm_g[l]
        b_full = jnp.repeat(b_spatial[l].T, GMLP_GROUP_DIM, axis=1)
        mod3 = _ada(c_all, w_ada, b_ada, l, d // 2).reshape(ADA_ROWS, N_MOD, d)

        mod8 = jnp.broadcast_to(mod3[:, :, None, :], (ADA_ROWS, N_MOD, SUBLANES, d))
        g8 = jnp.broadcast_to(g[:, None, :], (g.shape[0], SUBLANES, d))

        wf_in0, wf_out0 = w_ffn_in[l, 0].astype(BF16), w_ffn_out[l, 0].astype(BF16)
        xt = _ffn(xt, mod8, g8, wf_in0, wf_out0, 0, x_row, tm_ffn, tf_ffn)
        ct = _ffn(ct, mod8, g8, wf_in0, wf_out0, 0, c_row, tm_ctx, tf_ffn)

        q, kv, ug, gates = _proj(xt, mod3, g, cos, sin_lo, sin_hi, w_in_b, l, seq, 256)
        kv_ctx = _ctx_kv(ct, mod3, g, w_in_b, l, ctx_row, min(256, bsz * c_len))
        mix_b, wf_in1, wf_out1 = _attn(q, kv, kv_ctx, attn_sink, l, bsz, seq, c_len,
                                       [(w_ffn_in, (l, 1)), (w_ffn_out, (l, 1))])
        xt = _merge(xt, mod3, g, ug, gmlp_ln_g, gmlp_ln_b, w_s, b_full, mix_b, gates, w_a, w_b, w_o, l, seq, 256)

        xt = _ffn(xt, mod8, g8, wf_in1, wf_out1, 2, x_row, tm_ffn, tf_ffn)
    return xt.reshape(bsz, seq, d)
```

```python
import functools

import jax
import jax.numpy as jnp
from jax import lax
from jax.experimental import pallas as pl
from jax.experimental.pallas import tpu as pltpu

GRID_W = 64
HEAD_DIM = 128
N_HEADS = 8
N_KV_HEADS = 2
HEADS_PER_KV = N_HEADS // N_KV_HEADS
ATTN_WIDTH = N_HEADS * HEAD_DIM
KV_WIDTH = N_KV_HEADS * HEAD_DIM
WINDOW = 128
ATTN_BLOCK = 128
ROPE_BASE = 10000.0
CHUNK = 128
N_GMLP_GROUPS = 8
GMLP_WIDTH = 1024
GMLP_GROUP_DIM = GMLP_WIDTH // N_GMLP_GROUPS
FFN_RES_WEIGHT = 0.5
N_MOD = 9
NORM_EPS = 1e-6
NEG_INF = -1e30

ADA_ROWS = 8
V7X_VMEM_LIMIT = 58 * 1024 * 1024
PROJ_COLS = 512
SUBLANES = 8
ROW_CHUNK = 16
ROW_CHUNK_UNROLL = 8
STATS_UNROLL = 16
FFN_ACT_COLS = 256
ATTN_ROW_CHUNK = 32

F32 = jnp.float32
BF16 = jnp.bfloat16


def _params(*sem):
    return pltpu.CompilerParams(dimension_semantics=sem, vmem_limit_bytes=V7X_VMEM_LIMIT)


def _dot(a, b):
    return jnp.dot(a, b, preferred_element_type=F32)


def _rms(t):
    return t * lax.rsqrt(jnp.mean(t * t, axis=-1, keepdims=True) + NORM_EPS)


def _chunk_rows(v8):
    return jnp.concatenate([v8] * (ROW_CHUNK // SUBLANES), axis=0)


def _cast_specs(cast_srcs, n_steps, step_of):
    in_specs, out_specs, out_shapes = [], [], []
    for a, lead in cast_srcs:
        r, w = a.shape[len(lead):]
        rb = next(c for c in range(ROW_CHUNK, r + 1, ROW_CHUNK) if r % c == 0 and r // c <= n_steps)
        slab = lambda *idx, last=r // rb - 1: jnp.minimum(step_of(*idx), last)
        in_specs.append(pl.BlockSpec((None,) * len(lead) + (rb, w),
                                     lambda *idx, lead=lead, slab=slab: (*lead, slab(*idx), 0)))
        out_specs.append(pl.BlockSpec((rb, w), lambda *idx, slab=slab: (slab(*idx), 0)))
        out_shapes.append(jax.ShapeDtypeStruct((r, w), BF16))
    return in_specs, out_specs, out_shapes


def _cast_slabs(cast_src, cast_dst):
    for src, dst in zip(cast_src, cast_dst):
        dst[...] = src[...].astype(BF16)


def _for_row_chunks(n_rows, body, unroll=ROW_CHUNK_UNROLL):
    def step(k, carry):
        body(pl.ds(pl.multiple_of(k * ROW_CHUNK, ROW_CHUNK), ROW_CHUNK))
        return carry
    lax.fori_loop(0, n_rows // ROW_CHUNK, step, 0, unroll=unroll)


def _ada_kernel(c_ref, w_ref, b_ref, o_ref):
    c = c_ref[...]
    a = (c * jax.nn.sigmoid(c)).astype(BF16)
    o_ref[...] = _dot(a, w_ref[...].astype(BF16)) + b_ref[...]


def _ada(c_all, w_ada, b_ada, lyr, tn):
    _, d, n = w_ada.shape
    assert n % tn == 0
    return pl.pallas_call(
        _ada_kernel,
        grid=(n // tn,),
        in_specs=[pl.BlockSpec((ADA_ROWS, d), lambda j: (0, 0)),
                  pl.BlockSpec((None, d, tn), lambda j: (lyr, 0, j)),
                  pl.BlockSpec((1, tn), lambda j: (lyr, j))],
        out_specs=pl.BlockSpec((ADA_ROWS, tn), lambda j: (0, j)),
        out_shape=jax.ShapeDtypeStruct((ADA_ROWS, n), F32),
        compiler_params=_params("arbitrary"),
        name="ada",
    )(c_all, w_ada, b_ada)


def _ffn_kernel(sub, n_cast, t_ref, mod_ref, g_ref, wg_ref, wu_ref, wo_ref, *rest):
    cast_src, (o_ref, *cast_dst), (h_ref, rstd_ref) = rest[:n_cast], rest[n_cast:2 * n_cast + 1], rest[2 * n_cast + 1:]
    j = pl.program_id(1)
    tm = t_ref.shape[0]
    _cast_slabs(cast_src, cast_dst)

    @pl.when(j == 0)
    def _():
        def body(rows):
            y = _rms(t_ref[rows, :]) * _chunk_rows(g_ref[2 * sub])
            h = y * _chunk_rows(1.0 + mod_ref[0, 3 * sub + 1]) + _chunk_rows(mod_ref[0, 3 * sub])
            h_ref[rows, :] = h.astype(BF16)
            o_ref[rows, :] = jnp.zeros((ROW_CHUNK, o_ref.shape[1]), F32)
        _for_row_chunks(tm, body)

    h = h_ref[...]
    tf = wg_ref.shape[1]
    acts = []
    for c0 in range(0, tf, FFN_ACT_COLS):
        gt = _dot(h, wg_ref[:, c0:c0 + FFN_ACT_COLS])
        up = _dot(h, wu_ref[:, c0:c0 + FFN_ACT_COLS])
        acts.append(((gt * jax.nn.sigmoid(gt)) * up).astype(BF16))
    o_ref[...] += _dot(jnp.concatenate(acts, axis=1), wo_ref[...])

    @pl.when(j == pl.num_programs(1) - 1)
    def _():
        def stats(rows):
            a = o_ref[rows, :]
            rstd_ref[rows, :] = lax.rsqrt(jnp.mean(a * a, axis=-1, keepdims=True) + NORM_EPS)
        _for_row_chunks(tm, stats, unroll=STATS_UNROLL)

        def finish(rows):
            y = (o_ref[rows, :] * rstd_ref[rows, :]) * _chunk_rows(g_ref[2 * sub + 1])
            o_ref[rows, :] = t_ref[rows, :] + _chunk_rows(FFN_RES_WEIGHT * mod_ref[0, 3 * sub + 2]) * y
        _for_row_chunks(tm, finish)


def _ffn(t, mod8, g8, w_in, w_out, sub, mod_row, tm, tf, cast_srcs=()):
    n_tok, d = t.shape
    f = w_out.shape[0]
    nf = f // tf
    cast_in_specs, cast_out_specs, cast_shapes = _cast_specs(cast_srcs, (n_tok // tm) * nf, lambda i, j: i * nf + j)
    return pl.pallas_call(
        functools.partial(_ffn_kernel, sub, len(cast_srcs)),
        grid=(n_tok // tm, nf),
        in_specs=[pl.BlockSpec((tm, d), lambda i, j: (i, 0)),
                  pl.BlockSpec((1, N_MOD, SUBLANES, d), lambda i, j: (mod_row(i), 0, 0, 0)),
                  pl.BlockSpec(g8.shape, lambda i, j: (0, 0, 0)),
                  pl.BlockSpec((d, tf), lambda i, j: (0, j)),
                  pl.BlockSpec((d, tf), lambda i, j: (0, nf + j)),
                  pl.BlockSpec((tf, d), lambda i, j: (j, 0))] + cast_in_specs,
        out_specs=[pl.BlockSpec((tm, d), lambda i, j: (i, 0))] + cast_out_specs,
        out_shape=[jax.ShapeDtypeStruct((n_tok, d), F32)] + cast_shapes,
        scratch_shapes=[pltpu.VMEM((tm, d), BF16), pltpu.VMEM((tm, 1), F32)],
        compiler_params=_params("arbitrary", "arbitrary"),
        name=f"ffn{sub}",
    )(t, mod8, g8, w_in, w_in, w_out, *[a for a, _ in cast_srcs])


def _rope(x, cos, sin_lo, sin_hi):
    return (x * cos + pltpu.roll(x, HEAD_DIM - HEAD_DIM // 4, 1) * sin_lo
            + pltpu.roll(x, HEAD_DIM // 4, 1) * sin_hi)


def _modulated(t_ref, mod_ref, g_ref):
    y = _rms(t_ref[...]) * g_ref[2:3, :]
    return y * (1.0 + mod_ref[0, 4:5, :]) + mod_ref[0, 3:4, :]


def _proj_kernel(t_ref, mod_ref, g_ref, cos_ref, slo_ref, shi_ref, w_ref,
                 q_ref, kv_ref, ug_ref, gate_ref, h_ref):
    h_ref[...] = _modulated(t_ref, mod_ref, g_ref).astype(BF16)
    tn = PROJ_COLS

    def cols(c):
        return _dot(h_ref[...], w_ref[:, c * tn:(c + 1) * tn])

    def rope_into(o_ref, p, col0, n_heads):
        cos, slo, shi = cos_ref[...], slo_ref[...], shi_ref[...]
        for a in range(n_heads):
            x = p[:, a * HEAD_DIM:(a + 1) * HEAD_DIM]
            o_ref[:, col0 + a * HEAD_DIM:col0 + (a + 1) * HEAD_DIM] = _rope(x, cos, slo, shi).astype(BF16)

    c = 0
    for k in range(ATTN_WIDTH // tn):
        rope_into(q_ref, cols(c), k * tn, tn // HEAD_DIM)
        c += 1
    p = cols(c)
    rope_into(kv_ref, p, 0, N_KV_HEADS)
    kv_ref[:, KV_WIDTH:] = p[:, KV_WIDTH:].astype(BF16)
    c += 1
    for k in range(ug_ref.shape[1] // tn):
        ug_ref[:, k * tn:(k + 1) * tn] = jax.nn.gelu(cols(c)).astype(BF16)
        c += 1
    for k in range(gate_ref.shape[1] // tn):
        gate_ref[:, k * tn:(k + 1) * tn] = jax.nn.sigmoid(cols(c)).astype(BF16)
        c += 1


def _proj(x, mod3, norm_g, cos, sin_lo, sin_hi, w_in, lyr, seq, tm):
    n_tok, d = x.shape
    p_width = w_in.shape[2]
    assert PROJ_COLS == 2 * KV_WIDTH and ATTN_WIDTH % PROJ_COLS == 0 and d % PROJ_COLS == 0
    assert p_width == ATTN_WIDTH + 2 * KV_WIDTH + 2 * GMLP_WIDTH + 2 * d
    tiles_per_seq = seq // tm
    hd_spec = pl.BlockSpec((tm, HEAD_DIM), lambda i: (i % tiles_per_seq, 0))
    widths = (ATTN_WIDTH, 2 * KV_WIDTH, 2 * GMLP_WIDTH, 2 * d)
    return pl.pallas_call(
        _proj_kernel,
        grid=(n_tok // tm,),
        in_specs=[pl.BlockSpec((tm, d), lambda i: (i, 0)),
                  pl.BlockSpec((1, N_MOD, d), lambda i: (i // tiles_per_seq, 0, 0)),
                  pl.BlockSpec(norm_g.shape, lambda i: (0, 0)),
                  hd_spec, hd_spec, hd_spec,
                  pl.BlockSpec((None, d, p_width), lambda i: (lyr, 0, 0), pipeline_mode=pl.Buffered(1))],
        out_specs=[pl.BlockSpec((tm, w), lambda i: (i, 0)) for w in widths],
        out_shape=[jax.ShapeDtypeStruct((n_tok, w), BF16) for w in widths],
        scratch_shapes=[pltpu.VMEM((tm, d), BF16)],
        compiler_params=_params("parallel"),
        name="proj",
    )(x, mod3, norm_g, cos, sin_lo, sin_hi, w_in)


def _ctx_kv_kernel(t_ref, mod_ref, g_ref, w_ref, kv_ref):
    kv_ref[...] = _dot(_modulated(t_ref, mod_ref, g_ref).astype(BF16), w_ref[...]).astype(BF16)


def _ctx_kv(ctx, mod3, norm_g, w_in, lyr, ctx_row, tm):
    n_tok, d = ctx.shape
    tn = 2 * KV_WIDTH
    return pl.pallas_call(
        _ctx_kv_kernel,
        grid=(n_tok // tm,),
        in_specs=[pl.BlockSpec((tm, d), lambda i: (i, 0)),
                  pl.BlockSpec((1, N_MOD, d), lambda i: (ctx_row, 0, 0)),
                  pl.BlockSpec(norm_g.shape, lambda i: (0, 0)),
                  pl.BlockSpec((None, d, tn), lambda i: (lyr, 0, ATTN_WIDTH // tn))],
        out_specs=pl.BlockSpec((tm, tn), lambda i: (i, 0)),
        out_shape=jax.ShapeDtypeStruct((n_tok, tn), BF16),
        compiler_params=_params("parallel"),
        name="ctx_kv",
    )(ctx, mod3, norm_g, w_in)


def _gmlp_rows(u_ref, v_ref, lg_ref, lb_ref, ws_ref, bs_ref, o_ref):
    tm = u_ref.shape[0]
    for c in range(tm // CHUNK):
        rows = slice(c * CHUNK, (c + 1) * CHUNK)
        v = v_ref[rows, :].astype(F32)
        mu = jnp.mean(v, axis=-1, keepdims=True)
        vc = v - mu
        var = jnp.mean(vc * vc, axis=-1, keepdims=True)
        vn = ((vc * lax.rsqrt(var + NORM_EPS)) * lg_ref[...] + lb_ref[...]).astype(BF16)
        for g in range(N_GMLP_GROUPS):
            cols = slice(g * GMLP_GROUP_DIM, (g + 1) * GMLP_GROUP_DIM)
            mixed = _dot(ws_ref[g], vn[:, cols]) + bs_ref[:, cols]
            o_ref[rows, cols] = (u_ref[rows, cols].astype(F32) * mixed).astype(BF16)


def _attn_kernel(n_cast, sink_ref, q_ref, kvp_ref, kvc_ref, kvn_ref, kvx_ref, bprev_ref, bnext_ref, *rest):
    cast_src, (o_ref, *cast_dst), (s_ref, p_ref) = rest[:n_cast], rest[n_cast:2 * n_cast + 1], rest[2 * n_cast + 1:]
    _cast_slabs(cast_src, cast_dst)
    blk_q = q_ref.shape[0]
    rows = HEADS_PER_KV * blk_q
    rc = ATTN_ROW_CHUNK
    kv_refs = (kvp_ref, kvc_ref, kvn_ref, kvx_ref)
    log2e = 1.4426950408889634
    scale2 = HEAD_DIM ** -0.5 * log2e
    nt = (((1,), (1,)), ((), ()))
    n_pieces = s_ref.shape[2] // ATTN_BLOCK
    group_heads = [range(g * HEADS_PER_KV, (g + 1) * HEADS_PER_KV) for g in range(N_KV_HEADS)]

    for g, heads in enumerate(group_heads):
        kc = slice(g * HEAD_DIM, (g + 1) * HEAD_DIM)
        q = jnp.concatenate([q_ref[:, a * HEAD_DIM:(a + 1) * HEAD_DIM] for a in heads], axis=0)
        k_all = jnp.concatenate([r[:, kc] for r in kv_refs], axis=0)
        s_ref[g] = lax.dot_general(q, k_all, nt, preferred_element_type=F32)

    for g, heads in enumerate(group_heads):
        invs = []
        for r in range(0, rows, rc):
            sink2 = sink_ref[heads[r // blk_q]] * log2e
            rq = slice(r % blk_q, r % blk_q + rc)
            pieces = [s_ref[g, r:r + rc, n * ATTN_BLOCK:(n + 1) * ATTN_BLOCK] for n in range(n_pieces)]
            pieces[0] = pieces[0] + bprev_ref[rq, :]
            pieces[2] = pieces[2] + bnext_ref[rq, :]
            m = jnp.max(functools.reduce(jnp.maximum, pieces), axis=-1, keepdims=True)
            m2 = jnp.maximum(m * scale2, sink2)
            es = [jnp.exp2(p * scale2 - m2) for p in pieces]
            denom = jnp.sum(functools.reduce(jnp.add, es), axis=-1, keepdims=True) + jnp.exp2(sink2 - m2)
            invs.append(1.0 / denom)
            for n, e in enumerate(es):
                p_ref[g, r:r + rc, n * ATTN_BLOCK:(n + 1) * ATTN_BLOCK] = e.astype(BF16)
        vc = slice(KV_WIDTH + g * HEAD_DIM, KV_WIDTH + (g + 1) * HEAD_DIM)
        v_all = jnp.concatenate([r[:, vc] for r in kv_refs], axis=0)
        out = _dot(p_ref[g], v_all) * jnp.concatenate(invs, axis=0)
        for r, a in enumerate(heads):
            o_ref[:, a * HEAD_DIM:(a + 1) * HEAD_DIM] = out[r * blk_q:(r + 1) * blk_q, :].astype(BF16)


def _band_bias():
    qi = jnp.arange(ATTN_BLOCK)[:, None]
    kj = jnp.arange(ATTN_BLOCK)[None, :]
    neg = jnp.full((ATTN_BLOCK, ATTN_BLOCK), NEG_INF, F32)
    prev = jnp.where(kj >= qi, 0.0, NEG_INF).astype(F32)
    nxt = jnp.where(kj <= qi, 0.0, NEG_INF).astype(F32)
    return jnp.stack([jnp.stack([prev, neg]), jnp.stack([nxt, neg])])


def _attn(q, kv, kv_ctx, sink, lyr, bsz, seq, c_len, cast_srcs):
    n_tok = q.shape[0]
    nb = seq // ATTN_BLOCK
    kv_w = 2 * KV_WIDTH
    assert c_len % ATTN_BLOCK == 0 and WINDOW == ATTN_BLOCK
    n_keys = 3 * ATTN_BLOCK + c_len
    rows = HEADS_PER_KV * ATTN_BLOCK
    bias = _band_bias()
    bias_block = (None, None, ATTN_BLOCK, ATTN_BLOCK)
    cast_in_specs, cast_out_specs, cast_shapes = _cast_specs(cast_srcs, bsz * nb, lambda b, k: b * nb + k)
    return pl.pallas_call(
        functools.partial(_attn_kernel, len(cast_srcs)),
        grid=(bsz, nb),
        in_specs=[pl.BlockSpec(memory_space=pltpu.SMEM),
                  pl.BlockSpec((ATTN_BLOCK, ATTN_WIDTH), lambda b, k: (b * nb + k, 0)),
                  pl.BlockSpec((ATTN_BLOCK, kv_w), lambda b, k: (b * nb + jnp.maximum(k - 1, 0), 0)),
                  pl.BlockSpec((ATTN_BLOCK, kv_w), lambda b, k: (b * nb + k, 0)),
                  pl.BlockSpec((ATTN_BLOCK, kv_w), lambda b, k: (b * nb + jnp.minimum(k + 1, nb - 1), 0)),
                  pl.BlockSpec((c_len, kv_w), lambda b, k: (b, 0)),
                  pl.BlockSpec(bias_block, lambda b, k: (0, (k == 0).astype(jnp.int32), 0, 0)),
                  pl.BlockSpec(bias_block, lambda b, k: (1, (k == nb - 1).astype(jnp.int32), 0, 0))]
                 + cast_in_specs,
        out_specs=[pl.BlockSpec((ATTN_BLOCK, ATTN_WIDTH), lambda b, k: (b * nb + k, 0))] + cast_out_specs,
        out_shape=[jax.ShapeDtypeStruct((n_tok, ATTN_WIDTH), BF16)]
                  + cast_shapes,
        scratch_shapes=[pltpu.VMEM((N_KV_HEADS, rows, n_keys), F32),
                        pltpu.VMEM((N_KV_HEADS, rows, n_keys), BF16)],
        compiler_params=_params("arbitrary", "arbitrary"),
        name="attn",
    )(sink[lyr], q, kv, kv, kv, kv_ctx, bias, bias, *[a for a, _ in cast_srcs])


def _merge_kernel(x_ref, mod_ref, g_ref, u_ref, v_ref, lg_ref, lb_ref, ws_ref, bs_ref, b_ref, ga_ref, gb_ref,
                  wa_ref, wb_ref, wo_ref, o_ref, a_ref):
    branch_b = gb_ref[...].astype(F32) * _dot(b_ref[...], wb_ref[...])
    _gmlp_rows(u_ref, v_ref, lg_ref, lb_ref, ws_ref, bs_ref, a_ref)
    merged = ga_ref[...].astype(F32) * _dot(a_ref[...], wa_ref[...]) + branch_b
    y = _dot(merged.astype(BF16), wo_ref[...])
    o_ref[...] = x_ref[...] + mod_ref[0, 5:6, :] * (_rms(y) * g_ref[3:4, :])


def _merge(x, mod3, norm_g, ug, ln_g, ln_b, w_s, b_full, mix_b, gates, w_a, w_b, w_o, lyr, seq, tm):
    n_tok, d = x.shape
    assert tm % CHUNK == 0
    tiles_per_seq = seq // tm
    weight = lambda w: pl.BlockSpec((None,) + w.shape[1:], lambda i: (lyr,) + (0,) * (w.ndim - 1),
                                    pipeline_mode=pl.Buffered(1))
    return pl.pallas_call(
        _merge_kernel,
        grid=(n_tok // tm,),
        in_specs=[pl.BlockSpec((tm, d), lambda i: (i, 0)),
                  pl.BlockSpec((1, N_MOD, d), lambda i: (i // tiles_per_seq, 0, 0)),
                  pl.BlockSpec(norm_g.shape, lambda i: (0, 0)),
                  pl.BlockSpec((tm, GMLP_WIDTH), lambda i: (i, 0)),
                  pl.BlockSpec((tm, GMLP_WIDTH), lambda i: (i, 1)),
                  pl.BlockSpec((1, GMLP_WIDTH), lambda i: (lyr, 0)),
                  pl.BlockSpec((1, GMLP_WIDTH), lambda i: (lyr, 0)),
                  weight(w_s),
                  pl.BlockSpec(b_full.shape, lambda i: (0, 0)),
                  pl.BlockSpec((tm, ATTN_WIDTH), lambda i: (i, 0)),
                  pl.BlockSpec((tm, d), lambda i: (i, 0)),
                  pl.BlockSpec((tm, d), lambda i: (i, 1)),
                  weight(w_a), weight(w_b), weight(w_o)],
        out_specs=pl.BlockSpec((tm, d), lambda i: (i, 0)),
        out_shape=jax.ShapeDtypeStruct((n_tok, d), F32),
        scratch_shapes=[pltpu.VMEM((tm, GMLP_WIDTH), BF16)],
        compiler_params=_params("parallel"),
        name="merge",
    )(x, mod3, norm_g, ug, ug, ln_g, ln_b, w_s, b_full, mix_b, gates, gates, w_a, w_b, w_o)


def _rope_tables(seq):
    rows = seq // GRID_W
    row = jnp.repeat(jnp.arange(rows), GRID_W).astype(F32)
    col = jnp.tile(jnp.arange(GRID_W), rows).astype(F32)
    quarter = HEAD_DIM // 4
    freqs = ROPE_BASE ** (-jnp.arange(quarter, dtype=F32) / quarter)
    ang_r = row[:, None] * freqs[None, :]
    ang_c = col[:, None] * freqs[None, :]
    ang = jnp.concatenate([ang_r, ang_r, ang_c, ang_c], axis=-1)
    cos, sin = jnp.cos(ang), jnp.sin(ang)
    first = (jnp.arange(HEAD_DIM) // quarter) % 2 == 0
    return cos, jnp.where(first, -sin, 0.0), jnp.where(first, 0.0, sin)


def kernel(x, c, ctx, c_ctx, w_ada, b_ada, norm_g, w_ffn_in, w_ffn_out, w_in, gmlp_ln_g, gmlp_ln_b,
           w_spatial, b_spatial, attn_sink, w_branch_a, w_branch_b, w_out):
    bsz, seq, d = x.shape
    c_len = ctx.shape[1]
    depth = w_ada.shape[0]
    ctx_row = bsz
    assert bsz < ADA_ROWS and seq % 512 == 0 and (bsz * c_len) % 256 == 0
    assert depth == 1, "context-stream outputs of a non-final layer are not implemented"

    cos, sin_lo, sin_hi = _rope_tables(seq)
    c_all = jnp.zeros((ADA_ROWS, d), F32).at[:bsz].set(c).at[ctx_row].set(c_ctx)
    xt = x.reshape(bsz * seq, d)
    ct = ctx.reshape(bsz * c_len, d)
    tm_ffn = 1024
    tf_ffn = 512
    tm_ctx = min(1024, bsz * c_len)
    assert seq % tm_ffn == 0 and (bsz * c_len) % tm_ctx == 0
    x_row = lambda i: i // (seq // tm_ffn)
    c_row = lambda i: ctx_row

    w_s = w_spatial.astype(BF16)

    for l in range(depth):
        g = norm_g[l]
        b_full = jnp.repeat(b_spatial[l].T, GMLP_GROUP_DIM, axis=1)
        mod3 = _ada(c_all, w_ada, b_ada, l, d // 2).reshape(ADA_ROWS, N_MOD, d)

        mod8 = jnp.broadcast_to(mod3[:, :, None, :], (ADA_ROWS, N_MOD, SUBLANES, d))
        g8 = jnp.broadcast_to(g[:, None, :], (g.shape[0], SUBLANES, d))

        wf_in0, wf_out0 = w_ffn_in[l, 0].astype(BF16), w_ffn_out[l, 0].astype(BF16)
        xt, w_in_b = _ffn(xt, mod8, g8, wf_in0, wf_out0, 0, x_row, tm_ffn, tf_ffn, [(w_in, (l,))])
        ct, = _ffn(ct, mod8, g8, wf_in0, wf_out0, 0, c_row, tm_ctx, tf_ffn)

        q, kv, ug, gates = _proj(xt, mod3, g, cos, sin_lo, sin_hi, w_in_b[None], 0, seq, 256)
        kv_ctx = _ctx_kv(ct, mod3, g, w_in_b[None], 0, ctx_row, min(256, bsz * c_len))
        mix_b, wf_in1, wf_out1, w_a, w_b, w_o = _attn(
            q, kv, kv_ctx, attn_sink, l, bsz, seq, c_len,
            [(w_ffn_in, (l, 1)), (w_ffn_out, (l, 1)), (w_branch_a, (l,)), (w_branch_b, (l,)), (w_out, (l,))])
        xt = _merge(xt, mod3, g, ug, gmlp_ln_g[l][None], gmlp_ln_b[l][None], w_s[l][None], b_full, mix_b, gates,
                    w_a[None], w_b[None], w_o[None], 0, seq, 256)

        xt, = _ffn(xt, mod8, g8, wf_in1, wf_out1, 2, x_row, tm_ffn, tf_ffn)
    return xt.reshape(bsz, seq, d)
```

```python
import functools

import jax
import jax.numpy as jnp
from jax import lax
from jax.experimental import pallas as pl
from jax.experimental.pallas import tpu as pltpu

GRID_W = 64
HEAD_DIM = 128
N_HEADS = 8
N_KV_HEADS = 2
HEADS_PER_KV = N_HEADS // N_KV_HEADS
ATTN_WIDTH = N_HEADS * HEAD_DIM
KV_WIDTH = N_KV_HEADS * HEAD_DIM
WINDOW = 128
ATTN_BLOCK = 128
ROPE_BASE = 10000.0
CHUNK = 128
N_GMLP_GROUPS = 8
GMLP_WIDTH = 1024
GMLP_GROUP_DIM = GMLP_WIDTH // N_GMLP_GROUPS
FFN_RES_WEIGHT = 0.5
N_MOD = 9
NORM_EPS = 1e-6
NEG_INF = -1e30

ADA_ROWS = 8
V7X_VMEM_LIMIT = 58 * 1024 * 1024
PROJ_COLS = 512
SUBLANES = 8
ROW_CHUNK = 16
ROW_CHUNK_UNROLL = 32
STATS_UNROLL = 32
FFN_ACT_COLS = 256
ATTN_ROW_CHUNK = 32

F32 = jnp.float32
BF16 = jnp.bfloat16


def _params(*sem):
    return pltpu.CompilerParams(dimension_semantics=sem, vmem_limit_bytes=V7X_VMEM_LIMIT)


def _dot(a, b):
    return jnp.dot(a, b, preferred_element_type=F32)


def _rms(t):
    return t * lax.rsqrt(jnp.mean(t * t, axis=-1, keepdims=True) + NORM_EPS)


def _chunk_rows(v8):
    return jnp.concatenate([v8] * (ROW_CHUNK // SUBLANES), axis=0)


def _cast_specs(cast_srcs, n_steps, step_of):
    in_specs, out_specs, out_shapes = [], [], []
    for a, lead in cast_srcs:
        r, w = a.shape[len(lead):]
        rb = next(c for c in range(ROW_CHUNK, r + 1, ROW_CHUNK) if r % c == 0 and r // c <= n_steps)
        slab = lambda *idx, last=r // rb - 1: jnp.minimum(step_of(*idx), last)
        in_specs.append(pl.BlockSpec((None,) * len(lead) + (rb, w),
                                     lambda *idx, lead=lead, slab=slab: (*lead, slab(*idx), 0)))
        out_specs.append(pl.BlockSpec((rb, w), lambda *idx, slab=slab: (slab(*idx), 0)))
        out_shapes.append(jax.ShapeDtypeStruct((r, w), BF16))
    return in_specs, out_specs, out_shapes


def _cast_slabs(cast_src, cast_dst):
    for src, dst in zip(cast_src, cast_dst):
        dst[...] = src[...].astype(BF16)


def _for_row_chunks(n_rows, body, unroll=ROW_CHUNK_UNROLL):
    def step(k, carry):
        body(pl.ds(pl.multiple_of(k * ROW_CHUNK, ROW_CHUNK), ROW_CHUNK))
        return carry
    lax.fori_loop(0, n_rows // ROW_CHUNK, step, 0, unroll=unroll)


def _ada_kernel(c_ref, w_ref, b_ref, o_ref):
    c = c_ref[...]
    a = (c * jax.nn.sigmoid(c)).astype(BF16)
    o_ref[...] = _dot(a, w_ref[...].astype(BF16)) + b_ref[...]


def _ada(c_all, w_ada, b_ada, lyr, tn):
    _, d, n = w_ada.shape
    assert n % tn == 0
    return pl.pallas_call(
        _ada_kernel,
        grid=(n // tn,),
        in_specs=[pl.BlockSpec((ADA_ROWS, d), lambda j: (0, 0)),
                  pl.BlockSpec((None, d, tn), lambda j: (lyr, 0, j)),
                  pl.BlockSpec((1, tn), lambda j: (lyr, j))],
        out_specs=pl.BlockSpec((ADA_ROWS, tn), lambda j: (0, j)),
        out_shape=jax.ShapeDtypeStruct((ADA_ROWS, n), F32),
        compiler_params=_params("arbitrary"),
        name="ada",
    )(c_all, w_ada, b_ada)


def _ffn_kernel(sub, n_cast, t_ref, mod_ref, g_ref, wg_ref, wu_ref, wo_ref, *rest):
    cast_src, (o_ref, *cast_dst), (h_ref, rstd_ref) = rest[:n_cast], rest[n_cast:2 * n_cast + 1], rest[2 * n_cast + 1:]
    j = pl.program_id(1)
    tm = t_ref.shape[0]
    _cast_slabs(cast_src, cast_dst)

    @pl.when(j == 0)
    def _():
        def body(rows):
            y = _rms(t_ref[rows, :]) * _chunk_rows(g_ref[2 * sub])
            h = y * _chunk_rows(1.0 + mod_ref[0, 3 * sub + 1]) + _chunk_rows(mod_ref[0, 3 * sub])
            h_ref[rows, :] = h.astype(BF16)
            o_ref[rows, :] = jnp.zeros((ROW_CHUNK, o_ref.shape[1]), F32)
        _for_row_chunks(tm, body)

    h = h_ref[...]
    tf = wg_ref.shape[1]
    acts = []
    for c0 in range(0, tf, FFN_ACT_COLS):
        gt = _dot(h, wg_ref[:, c0:c0 + FFN_ACT_COLS])
        up = _dot(h, wu_ref[:, c0:c0 + FFN_ACT_COLS])
        acts.append(((gt * jax.nn.sigmoid(gt)) * up).astype(BF16))
    o_ref[...] += _dot(jnp.concatenate(acts, axis=1), wo_ref[...])

    @pl.when(j == pl.num_programs(1) - 1)
    def _():
        def stats(rows):
            a = o_ref[rows, :]
            rstd_ref[rows, :] = lax.rsqrt(jnp.mean(a * a, axis=-1, keepdims=True) + NORM_EPS)
        _for_row_chunks(tm, stats, unroll=STATS_UNROLL)

        def finish(rows):
            y = (o_ref[rows, :] * rstd_ref[rows, :]) * _chunk_rows(g_ref[2 * sub + 1])
            o_ref[rows, :] = t_ref[rows, :] + _chunk_rows(FFN_RES_WEIGHT * mod_ref[0, 3 * sub + 2]) * y
        _for_row_chunks(tm, finish)


def _ffn(t, mod8, g8, w_in, w_out, sub, mod_row, tm, tf, cast_srcs=()):
    n_tok, d = t.shape
    f = w_out.shape[0]
    nf = f // tf
    cast_in_specs, cast_out_specs, cast_shapes = _cast_specs(cast_srcs, (n_tok // tm) * nf, lambda i, j: i * nf + j)
    return pl.pallas_call(
        functools.partial(_ffn_kernel, sub, len(cast_srcs)),
        grid=(n_tok // tm, nf),
        in_specs=[pl.BlockSpec((tm, d), lambda i, j: (i, 0)),
                  pl.BlockSpec((1, N_MOD, SUBLANES, d), lambda i, j: (mod_row(i), 0, 0, 0)),
                  pl.BlockSpec(g8.shape, lambda i, j: (0, 0, 0)),
                  pl.BlockSpec((d, tf), lambda i, j: (0, j)),
                  pl.BlockSpec((d, tf), lambda i, j: (0, nf + j)),
                  pl.BlockSpec((tf, d), lambda i, j: (j, 0))] + cast_in_specs,
        out_specs=[pl.BlockSpec((tm, d), lambda i, j: (i, 0))] + cast_out_specs,
        out_shape=[jax.ShapeDtypeStruct((n_tok, d), F32)] + cast_shapes,
        scratch_shapes=[pltpu.VMEM((tm, d), BF16), pltpu.VMEM((tm, 1), F32)],
        compiler_params=_params("arbitrary", "arbitrary"),
        name=f"ffn{sub}",
    )(t, mod8, g8, w_in, w_in, w_out, *[a for a, _ in cast_srcs])


def _rope(x, cos, sin_lo, sin_hi):
    return (x * cos + pltpu.roll(x, HEAD_DIM - HEAD_DIM // 4, 1) * sin_lo
            + pltpu.roll(x, HEAD_DIM // 4, 1) * sin_hi)


def _modulated(t_ref, mod_ref, g_ref):
    y = _rms(t_ref[...]) * g_ref[2:3, :]
    return y * (1.0 + mod_ref[0, 4:5, :]) + mod_ref[0, 3:4, :]


def _proj_kernel(t_ref, mod_ref, g_ref, cos_ref, slo_ref, shi_ref, w_ref,
                 q_ref, kv_ref, ug_ref, gate_ref, h_ref):
    h_ref[...] = _modulated(t_ref, mod_ref, g_ref).astype(BF16)
    tn = PROJ_COLS

    def cols(c):
        return _dot(h_ref[...], w_ref[:, c * tn:(c + 1) * tn])

    def rope_into(o_ref, p, col0, n_heads):
        cos, slo, shi = cos_ref[...], slo_ref[...], shi_ref[...]
        for a in range(n_heads):
            x = p[:, a * HEAD_DIM:(a + 1) * HEAD_DIM]
            o_ref[:, col0 + a * HEAD_DIM:col0 + (a + 1) * HEAD_DIM] = _rope(x, cos, slo, shi).astype(BF16)

    c = 0
    for k in range(ATTN_WIDTH // tn):
        rope_into(q_ref, cols(c), k * tn, tn // HEAD_DIM)
        c += 1
    p = cols(c)
    rope_into(kv_ref, p, 0, N_KV_HEADS)
    kv_ref[:, KV_WIDTH:] = p[:, KV_WIDTH:].astype(BF16)
    c += 1
    for k in range(ug_ref.shape[1] // tn):
        ug_ref[:, k * tn:(k + 1) * tn] = jax.nn.gelu(cols(c)).astype(BF16)
        c += 1
    for k in range(gate_ref.shape[1] // tn):
        gate_ref[:, k * tn:(k + 1) * tn] = jax.nn.sigmoid(cols(c)).astype(BF16)
        c += 1


def _proj(x, mod3, norm_g, cos, sin_lo, sin_hi, w_in, lyr, seq, tm):
    n_tok, d = x.shape
    p_width = w_in.shape[2]
    assert PROJ_COLS == 2 * KV_WIDTH and ATTN_WIDTH % PROJ_COLS == 0 and d % PROJ_COLS == 0
    assert p_width == ATTN_WIDTH + 2 * KV_WIDTH + 2 * GMLP_WIDTH + 2 * d
    tiles_per_seq = seq // tm
    hd_spec = pl.BlockSpec((tm, HEAD_DIM), lambda i: (i % tiles_per_seq, 0))
    widths = (ATTN_WIDTH, 2 * KV_WIDTH, 2 * GMLP_WIDTH, 2 * d)
    return pl.pallas_call(
        _proj_kernel,
        grid=(n_tok // tm,),
        in_specs=[pl.BlockSpec((tm, d), lambda i: (i, 0)),
                  pl.BlockSpec((1, N_MOD, d), lambda i: (i // tiles_per_seq, 0, 0)),
                  pl.BlockSpec(norm_g.shape, lambda i: (0, 0)),
                  hd_spec, hd_spec, hd_spec,
                  pl.BlockSpec((None, d, p_width), lambda i: (lyr, 0, 0), pipeline_mode=pl.Buffered(1))],
        out_specs=[pl.BlockSpec((tm, w), lambda i: (i, 0)) for w in widths],
        out_shape=[jax.ShapeDtypeStruct((n_tok, w), BF16) for w in widths],
        scratch_shapes=[pltpu.VMEM((tm, d), BF16)],
        compiler_params=_params("parallel"),
        name="proj",
    )(x, mod3, norm_g, cos, sin_lo, sin_hi, w_in)


def _ctx_kv_kernel(t_ref, mod_ref, g_ref, w_ref, kv_ref):
    kv_ref[...] = _dot(_modulated(t_ref, mod_ref, g_ref).astype(BF16), w_ref[...]).astype(BF16)


def _ctx_kv(ctx, mod3, norm_g, w_in, lyr, ctx_row, tm):
    n_tok, d = ctx.shape
    tn = 2 * KV_WIDTH
    return pl.pallas_call(
        _ctx_kv_kernel,
        grid=(n_tok // tm,),
        in_specs=[pl.BlockSpec((tm, d), lambda i: (i, 0)),
                  pl.BlockSpec((1, N_MOD, d), lambda i: (ctx_row, 0, 0)),
                  pl.BlockSpec(norm_g.shape, lambda i: (0, 0)),
                  pl.BlockSpec((None, d, tn), lambda i: (lyr, 0, ATTN_WIDTH // tn))],
        out_specs=pl.BlockSpec((tm, tn), lambda i: (i, 0)),
        out_shape=jax.ShapeDtypeStruct((n_tok, tn), BF16),
        compiler_params=_params("parallel"),
        name="ctx_kv",
    )(ctx, mod3, norm_g, w_in)


def _gmlp_rows(u_ref, v_ref, lg_ref, lb_ref, ws_ref, bs_ref, o_ref):
    tm = u_ref.shape[0]
    for c in range(tm // CHUNK):
        rows = slice(c * CHUNK, (c + 1) * CHUNK)
        v = v_ref[rows, :].astype(F32)
        mu = jnp.mean(v, axis=-1, keepdims=True)
        vc = v - mu
        var = jnp.mean(vc * vc, axis=-1, keepdims=True)
        vn = ((vc * lax.rsqrt(var + NORM_EPS)) * lg_ref[...] + lb_ref[...]).astype(BF16)
        for g in range(N_GMLP_GROUPS):
            cols = slice(g * GMLP_GROUP_DIM, (g + 1) * GMLP_GROUP_DIM)
            mixed = _dot(ws_ref[g], vn[:, cols]) + bs_ref[:, cols]
            o_ref[rows, cols] = (u_ref[rows, cols].astype(F32) * mixed).astype(BF16)


def _attn_kernel(n_cast, sink_ref, q_ref, kvp_ref, kvc_ref, kvn_ref, kvx_ref, bprev_ref, bnext_ref, *rest):
    cast_src, (o_ref, *cast_dst), (s_ref, p_ref) = rest[:n_cast], rest[n_cast:2 * n_cast + 1], rest[2 * n_cast + 1:]
    _cast_slabs(cast_src, cast_dst)
    blk_q = q_ref.shape[0]
    rows = HEADS_PER_KV * blk_q
    rc = ATTN_ROW_CHUNK
    kv_refs = (kvp_ref, kvc_ref, kvn_ref, kvx_ref)
    log2e = 1.4426950408889634
    scale2 = HEAD_DIM ** -0.5 * log2e
    nt = (((1,), (1,)), ((), ()))
    n_pieces = s_ref.shape[2] // ATTN_BLOCK
    group_heads = [range(g * HEADS_PER_KV, (g + 1) * HEADS_PER_KV) for g in range(N_KV_HEADS)]

    for g, heads in enumerate(group_heads):
        kc = slice(g * HEAD_DIM, (g + 1) * HEAD_DIM)
        q = jnp.concatenate([q_ref[:, a * HEAD_DIM:(a + 1) * HEAD_DIM] for a in heads], axis=0)
        k_all = jnp.concatenate([r[:, kc] for r in kv_refs], axis=0)
        s_ref[g] = lax.dot_general(q, k_all, nt, preferred_element_type=F32)

    for g, heads in enumerate(group_heads):
        invs = []
        for r in range(0, rows, rc):
            sink2 = sink_ref[heads[r // blk_q]] * log2e
            rq = slice(r % blk_q, r % blk_q + rc)
            pieces = [s_ref[g, r:r + rc, n * ATTN_BLOCK:(n + 1) * ATTN_BLOCK] for n in range(n_pieces)]
            pieces[0] = pieces[0] + bprev_ref[rq, :]
            pieces[2] = pieces[2] + bnext_ref[rq, :]
            m = jnp.max(functools.reduce(jnp.maximum, pieces), axis=-1, keepdims=True)
            m2 = jnp.maximum(m * scale2, sink2)
            es = [jnp.exp2(p * scale2 - m2) for p in pieces]
            denom = jnp.sum(functools.reduce(jnp.add, es), axis=-1, keepdims=True) + jnp.exp2(sink2 - m2)
            invs.append(1.0 / denom)
            for n, e in enumerate(es):
                p_ref[g, r:r + rc, n * ATTN_BLOCK:(n + 1) * ATTN_BLOCK] = e.astype(BF16)
        vc = slice(KV_WIDTH + g * HEAD_DIM, KV_WIDTH + (g + 1) * HEAD_DIM)
        v_all = jnp.concatenate([r[:, vc] for r in kv_refs], axis=0)
        out = _dot(p_ref[g], v_all) * jnp.concatenate(invs, axis=0)
        for r, a in enumerate(heads):
            o_ref[:, a * HEAD_DIM:(a + 1) * HEAD_DIM] = out[r * blk_q:(r + 1) * blk_q, :].astype(BF16)


def _band_bias():
    qi = jnp.arange(ATTN_BLOCK)[:, None]
    kj = jnp.arange(ATTN_BLOCK)[None, :]
    neg = jnp.full((ATTN_BLOCK, ATTN_BLOCK), NEG_INF, F32)
    prev = jnp.where(kj >= qi, 0.0, NEG_INF).astype(F32)
    nxt = jnp.where(kj <= qi, 0.0, NEG_INF).astype(F32)
    return jnp.stack([jnp.stack([prev, neg]), jnp.stack([nxt, neg])])


def _attn(q, kv, kv_ctx, sink, lyr, bsz, seq, c_len, cast_srcs):
    n_tok = q.shape[0]
    nb = seq // ATTN_BLOCK
    kv_w = 2 * KV_WIDTH
    assert c_len % ATTN_BLOCK == 0 and WINDOW == ATTN_BLOCK
    n_keys = 3 * ATTN_BLOCK + c_len
    rows = HEADS_PER_KV * ATTN_BLOCK
    bias = _band_bias()
    bias_block = (None, None, ATTN_BLOCK, ATTN_BLOCK)
    cast_in_specs, cast_out_specs, cast_shapes = _cast_specs(cast_srcs, bsz * nb, lambda b, k: b * nb + k)
    return pl.pallas_call(
        functools.partial(_attn_kernel, len(cast_srcs)),
        grid=(bsz, nb),
        in_specs=[pl.BlockSpec(memory_space=pltpu.SMEM),
                  pl.BlockSpec((ATTN_BLOCK, ATTN_WIDTH), lambda b, k: (b * nb + k, 0)),
                  pl.BlockSpec((ATTN_BLOCK, kv_w), lambda b, k: (b * nb + jnp.maximum(k - 1, 0), 0)),
                  pl.BlockSpec((ATTN_BLOCK, kv_w), lambda b, k: (b * nb + k, 0)),
                  pl.BlockSpec((ATTN_BLOCK, kv_w), lambda b, k: (b * nb + jnp.minimum(k + 1, nb - 1), 0)),
                  pl.BlockSpec((c_len, kv_w), lambda b, k: (b, 0)),
                  pl.BlockSpec(bias_block, lambda b, k: (0, (k == 0).astype(jnp.int32), 0, 0)),
                  pl.BlockSpec(bias_block, lambda b, k: (1, (k == nb - 1).astype(jnp.int32), 0, 0))]
                 + cast_in_specs,
        out_specs=[pl.BlockSpec((ATTN_BLOCK, ATTN_WIDTH), lambda b, k: (b * nb + k, 0))] + cast_out_specs,
        out_shape=[jax.ShapeDtypeStruct((n_tok, ATTN_WIDTH), BF16)]
                  + cast_shapes,
        scratch_shapes=[pltpu.VMEM((N_KV_HEADS, rows, n_keys), F32),
                        pltpu.VMEM((N_KV_HEADS, rows, n_keys), BF16)],
        compiler_params=_params("arbitrary", "arbitrary"),
        name="attn",
    )(sink[lyr], q, kv, kv, kv, kv_ctx, bias, bias, *[a for a, _ in cast_srcs])


def _merge_kernel(x_ref, mod_ref, g_ref, u_ref, v_ref, lg_ref, lb_ref, ws_ref, bs_ref, b_ref, ga_ref, gb_ref,
                  wa_ref, wb_ref, wo_ref, o_ref, a_ref):
    branch_b = gb_ref[...].astype(F32) * _dot(b_ref[...], wb_ref[...])
    _gmlp_rows(u_ref, v_ref, lg_ref, lb_ref, ws_ref, bs_ref, a_ref)
    merged = ga_ref[...].astype(F32) * _dot(a_ref[...], wa_ref[...]) + branch_b
    y = _dot(merged.astype(BF16), wo_ref[...])
    o_ref[...] = x_ref[...] + mod_ref[0, 5:6, :] * (_rms(y) * g_ref[3:4, :])


def _merge(x, mod3, norm_g, ug, ln_g, ln_b, w_s, b_full, mix_b, gates, w_a, w_b, w_o, lyr, seq, tm):
    n_tok, d = x.shape
    assert tm % CHUNK == 0
    tiles_per_seq = seq // tm
    weight = lambda w: pl.BlockSpec((None,) + w.shape[1:], lambda i: (lyr,) + (0,) * (w.ndim - 1),
                                    pipeline_mode=pl.Buffered(1))
    return pl.pallas_call(
        _merge_kernel,
        grid=(n_tok // tm,),
        in_specs=[pl.BlockSpec((tm, d), lambda i: (i, 0)),
                  pl.BlockSpec((1, N_MOD, d), lambda i: (i // tiles_per_seq, 0, 0)),
                  pl.BlockSpec(norm_g.shape, lambda i: (0, 0)),
                  pl.BlockSpec((tm, GMLP_WIDTH), lambda i: (i, 0)),
                  pl.BlockSpec((tm, GMLP_WIDTH), lambda i: (i, 1)),
                  pl.BlockSpec((1, GMLP_WIDTH), lambda i: (lyr, 0)),
                  pl.BlockSpec((1, GMLP_WIDTH), lambda i: (lyr, 0)),
                  weight(w_s),
                  pl.BlockSpec(b_full.shape, lambda i: (0, 0)),
                  pl.BlockSpec((tm, ATTN_WIDTH), lambda i: (i, 0)),
                  pl.BlockSpec((tm, d), lambda i: (i, 0)),
                  pl.BlockSpec((tm, d), lambda i: (i, 1)),
                  weight(w_a), weight(w_b), weight(w_o)],
        out_specs=pl.BlockSpec((tm, d), lambda i: (i, 0)),
        out_shape=jax.ShapeDtypeStruct((n_tok, d), F32),
        scratch_shapes=[pltpu.VMEM((tm, GMLP_WIDTH), BF16)],
        compiler_params=_params("parallel"),
        name="merge",
    )(x, mod3, norm_g, ug, ug, ln_g, ln_b, w_s, b_full, mix_b, gates, gates, w_a, w_b, w_o)


def _rope_tables(seq):
    rows = seq // GRID_W
    row = jnp.repeat(jnp.arange(rows), GRID_W).astype(F32)
    col = jnp.tile(jnp.arange(GRID_W), rows).astype(F32)
    quarter = HEAD_DIM // 4
    freqs = ROPE_BASE ** (-jnp.arange(quarter, dtype=F32) / quarter)
    ang_r = row[:, None] * freqs[None, :]
    ang_c = col[:, None] * freqs[None, :]
    ang = jnp.concatenate([ang_r, ang_r, ang_c, ang_c], axis=-1)
    cos, sin = jnp.cos(ang), jnp.sin(ang)
    first = (jnp.arange(HEAD_DIM) // quarter) % 2 == 0
    return cos, jnp.where(first, -sin, 0.0), jnp.where(first, 0.0, sin)


def kernel(x, c, ctx, c_ctx, w_ada, b_ada, norm_g, w_ffn_in, w_ffn_out, w_in, gmlp_ln_g, gmlp_ln_b,
           w_spatial, b_spatial, attn_sink, w_branch_a, w_branch_b, w_out):
    bsz, seq, d = x.shape
    c_len = ctx.shape[1]
    depth = w_ada.shape[0]
    ctx_row = bsz
    assert bsz < ADA_ROWS and seq % 512 == 0 and (bsz * c_len) % 256 == 0
    assert depth == 1, "context-stream outputs of a non-final layer are not implemented"

    cos, sin_lo, sin_hi = _rope_tables(seq)
    c_all = jnp.zeros((ADA_ROWS, d), F32).at[:bsz].set(c).at[ctx_row].set(c_ctx)
    xt = x.reshape(bsz * seq, d)
    ct = ctx.reshape(bsz * c_len, d)
    tm_ffn = 1024
    tf_ffn = 512
    tm_ctx = min(1024, bsz * c_len)
    assert seq % tm_ffn == 0 and (bsz * c_len) % tm_ctx == 0
    x_row = lambda i: i // (seq // tm_ffn)
    c_row = lambda i: ctx_row

    w_s = w_spatial.astype(BF16)

    for l in range(depth):
        g = norm_g[l]
        b_full = jnp.repeat(b_spatial[l].T, GMLP_GROUP_DIM, axis=1)
        mod3 = _ada(c_all, w_ada, b_ada, l, d // 2).reshape(ADA_ROWS, N_MOD, d)

        mod8 = jnp.broadcast_to(mod3[:, :, None, :], (ADA_ROWS, N_MOD, SUBLANES, d))
        g8 = jnp.broadcast_to(g[:, None, :], (g.shape[0], SUBLANES, d))

        wf_in0, wf_out0 = w_ffn_in[l, 0].astype(BF16), w_ffn_out[l, 0].astype(BF16)
        xt, w_in_b = _ffn(xt, mod8, g8, wf_in0, wf_out0, 0, x_row, tm_ffn, tf_ffn, [(w_in, (l,))])
        ct, = _ffn(ct, mod8, g8, wf_in0, wf_out0, 0, c_row, tm_ctx, tf_ffn)

        q, kv, ug, gates = _proj(xt, mod3, g, cos, sin_lo, sin_hi, w_in_b[None], 0, seq, 256)
        kv_ctx = _ctx_kv(ct, mod3, g, w_in_b[None], 0, ctx_row, min(256, bsz * c_len))
        mix_b, wf_in1, wf_out1, w_a, w_b, w_o = _attn(
            q, kv, kv_ctx, attn_sink, l, bsz, seq, c_len,
            [(w_ffn_in, (l, 1)), (w_ffn_out, (l, 1)), (w_branch_a, (l,)), (w_branch_b, (l,)), (w_out, (l,))])
        xt = _merge(xt, mod3, g, ug, gmlp_ln_g[l][None], gmlp_ln_b[l][None], w_s[l][None], b_full, mix_b, gates,
                    w_a[None], w_b[None], w_o[None], 0, seq, 256)

        xt, = _ffn(xt, mod8, g8, wf_in1, wf_out1, 2, x_row, tm_ffn, tf_ffn)
    return xt.reshape(bsz, seq, d)
```
